```python
import math
import jax, jax.numpy as jnp
from jax import lax
import numpy as np

D_MODEL = 2048
BATCH = 2
SEQ = 8192
DEPTH = 4

HEAD_DIM = 64
SWA_Q_HEADS = D_MODEL // (2 * HEAD_DIM)
SWA_KV_HEADS = max(1, SWA_Q_HEADS // 8)
SWA_GROUP = SWA_Q_HEADS // SWA_KV_HEADS
SWA_WINDOW = 128
SWA_BLOCK = 128
MOBA_HEADS = D_MODEL // (2 * HEAD_DIM)
MOBA_BLOCK = 256
MOBA_TOPK = 3
MOBA_Q_CHUNK = 32
N_HEADS = SWA_Q_HEADS + MOBA_HEADS
D_SWA = SWA_Q_HEADS * HEAD_DIM
D_SWA_KV = SWA_KV_HEADS * HEAD_DIM
D_MOBA = MOBA_HEADS * HEAD_DIM
D_MIX = D_SWA + D_MOBA
D_QKV = D_SWA + 2 * D_SWA_KV + 3 * D_MOBA
SPLIT_POINTS = [D_SWA, D_SWA + D_SWA_KV, D_SWA + 2 * D_SWA_KV,
                D_SWA + 2 * D_SWA_KV + D_MOBA, D_SWA + 2 * D_SWA_KV + 2 * D_MOBA]
REL_BUCKETS = 32
REL_MAX_DISTANCE = 2048
N_GROUPS = 4
EXPERTS_PER_GROUP = 8
N_EXPERTS = N_GROUPS * EXPERTS_PER_GROUP
D_EXPERT = 512
EXPERT_TOPK = 2
EPS = 1e-6
NEG = -1e30
SCALE = 1.0 / math.sqrt(HEAD_DIM)

kernel_name = 'hybrid_swa_sink_moba_hmoe_adaln'


def rmsnorm(x, g):
    x32 = x.astype(jnp.float32)
    r = lax.rsqrt(jnp.mean(x32 * x32, axis=-1, keepdims=True) + EPS)
    return (x32 * r).astype(x.dtype) * g


def rel_bucket(dist):
    n = jnp.maximum(dist, 0)
    max_exact = REL_BUCKETS // 2
    nf = jnp.maximum(n, 1).astype(jnp.float32)
    large = max_exact + (jnp.log(nf / max_exact) / math.log(REL_MAX_DISTANCE / max_exact)
                         * (REL_BUCKETS - max_exact)).astype(jnp.int32)
    large = jnp.minimum(large, REL_BUCKETS - 1)
    return jnp.where(n < max_exact, n, large)


def sliding_window_attention(q, k, v, sinks, rel_bias_a):
    b, s = q.shape[0], q.shape[1]
    w = SWA_BLOCK
    nblk = s // w
    qb = q.reshape(b, nblk, w, SWA_KV_HEADS, SWA_GROUP, HEAD_DIM)

    def with_prev(t):
        tb = t.reshape(b, nblk, w, SWA_KV_HEADS, HEAD_DIM)
        prev = jnp.pad(tb, ((0, 0), (1, 0), (0, 0), (0, 0), (0, 0)))[:, :-1]
        return jnp.concatenate([prev, tb], axis=2)

    kc, vc = with_prev(k), with_prev(v)
    scores = jnp.einsum('bnqkgd,bnskd->bnkgqs', qb, kc).astype(jnp.float32) * SCALE
    kidx = jnp.arange(2 * w)
    dist = (w + jnp.arange(w))[:, None] - kidx[None, :]
    band = (dist >= 0) & (dist < SWA_WINDOW)
    start_ok = (jnp.arange(nblk)[:, None] * w - w + kidx[None, :]) >= 0
    valid = band[None] & start_ok[:, None, :]
    bias = rel_bias_a[rel_bucket(dist)].astype(jnp.float32)
    bias = jnp.transpose(bias, (2, 0, 1)).reshape(SWA_KV_HEADS, SWA_GROUP, w, 2 * w)
    logits = jnp.where(valid[None, :, None, None], scores + bias, NEG)
    sink = sinks.astype(jnp.float32).reshape(SWA_KV_HEADS, SWA_GROUP)[None, None, :, :, None, None]
    m = jnp.maximum(jnp.max(logits, axis=-1, keepdims=True), sink)
    e = jnp.exp(logits - m)
    p = e / (jnp.sum(e, axis=-1, keepdims=True) + jnp.exp(sink - m))
    out = jnp.einsum('bnkgqs,bnskd->bnqkgd', p.astype(v.dtype), vc)
    return out.reshape(b, s, D_SWA)


def moba_attention(q, k, v, rel_bias_b):
    b, s = q.shape[0], q.shape[1]
    nblk = -(-s // MOBA_BLOCK)
    s_pad = nblk * MOBA_BLOCK
    pad = ((0, 0), (0, s_pad - s), (0, 0), (0, 0))
    q, k, v = [jnp.transpose(jnp.pad(t, pad), (0, 2, 1, 3)) for t in (q, k, v)]
    kblk = k.reshape(b, MOBA_HEADS, nblk, MOBA_BLOCK, HEAD_DIM)
    vblk = v.reshape(b, MOBA_HEADS, nblk, MOBA_BLOCK, HEAD_DIM)
    kmean = jnp.mean(kblk.astype(jnp.float32), axis=3)
    gate = jnp.einsum('bhtd,bhnd->bhtn', q.astype(jnp.float32), kmean)
    cur = jnp.arange(s_pad) // MOBA_BLOCK
    past = jnp.arange(nblk)[None, :] < cur[:, None]
    gate = jnp.where(past, gate, NEG)
    n_sel = min(MOBA_TOPK, nblk)
    _, sel = lax.top_k(gate, n_sel)
    sel_ok = sel < cur[:, None]
    tab = rel_bias_b.T.astype(jnp.float32)
    n_chunk = s_pad // MOBA_Q_CHUNK

    def chunk_major(t):
        return jnp.moveaxis(t.reshape(b, MOBA_HEADS, n_chunk, MOBA_Q_CHUNK, *t.shape[3:]), 2, 0)

    xs = (chunk_major(q), chunk_major(sel), chunk_major(sel_ok),
          jnp.arange(n_chunk, dtype=jnp.int32) * MOBA_Q_CHUNK)
    bi = jnp.arange(b)[:, None, None, None]
    hi = jnp.arange(MOBA_HEADS)[None, :, None, None]
    pidx = jnp.arange(MOBA_BLOCK)

    def step(args):
        qc, selc, okc, t0 = args
        tq = t0 + jnp.arange(MOBA_Q_CHUNK)
        ksel = kblk[bi, hi, selc]
        vsel = vblk[bi, hi, selc]
        s_sel = jnp.einsum('bhqd,bhqjpd->bhqjp', qc, ksel).astype(jnp.float32) * SCALE
        kpos = selc[..., None] * MOBA_BLOCK + pidx
        bias_sel = tab[hi[..., None], rel_bucket(tq[None, None, :, None, None] - kpos)]
        s_sel = jnp.where(okc[..., None], s_sel + bias_sel, NEG)
        own0 = (t0 // MOBA_BLOCK) * MOBA_BLOCK
        kown = lax.dynamic_slice_in_dim(k, own0, MOBA_BLOCK, axis=2)
        vown = lax.dynamic_slice_in_dim(v, own0, MOBA_BLOCK, axis=2)
        s_own = jnp.einsum('bhqd,bhpd->bhqp', qc, kown).astype(jnp.float32) * SCALE
        dist = tq[:, None] - (own0 + pidx)[None, :]
        s_own = jnp.where(dist >= 0, s_own + tab[:, rel_bucket(dist)], NEG)
        logits = jnp.concatenate([s_sel.reshape(b, MOBA_HEADS, MOBA_Q_CHUNK, n_sel * MOBA_BLOCK), s_own], axis=-1)
        p = jax.nn.softmax(logits, axis=-1).astype(v.dtype)
        p_sel = p[..., :n_sel * MOBA_BLOCK].reshape(b, MOBA_HEADS, MOBA_Q_CHUNK, n_sel, MOBA_BLOCK)
        p_own = p[..., n_sel * MOBA_BLOCK:]
        return (jnp.einsum('bhqjp,bhqjpd->bhqd', p_sel, vsel)
                + jnp.einsum('bhqp,bhpd->bhqd', p_own, vown))

    out = lax.map(step, xs)
    out = jnp.moveaxis(out, 0, 2).reshape(b, MOBA_HEADS, s_pad, HEAD_DIM)[:, :, :s]
    return jnp.transpose(out, (0, 2, 1, 3)).reshape(b, s, D_MOBA)


def hier_moe(h, w_group, b_group, w_router, b_router, w_gate, w_up, w_down):
    b, s, d = h.shape
    ht = h.reshape(b * s, d)
    grp_logits = (ht @ w_group).astype(jnp.float32) + b_group
    grp_prob = jax.nn.softmax(grp_logits, axis=-1)
    p_grp, g_idx = lax.top_k(grp_prob, 1)
    exp_logits = ((ht @ w_router).astype(jnp.float32) + b_router).reshape(-1, N_GROUPS, EXPERTS_PER_GROUP)
    in_grp = jnp.take_along_axis(exp_logits, g_idx[:, :, None], axis=1)[:, 0]
    top_val, top_idx = lax.top_k(in_grp, EXPERT_TOPK)
    wts = p_grp * jax.nn.softmax(top_val, axis=-1)
    e_idx = g_idx * EXPERTS_PER_GROUP + top_idx
    gates = jnp.sum(jax.nn.one_hot(e_idx, N_EXPERTS, dtype=jnp.float32) * wts[..., None], axis=1).astype(ht.dtype)
    out = jnp.zeros_like(ht)
    for e in range(N_EXPERTS):
        hid = jax.nn.silu(ht @ w_gate[e]) * (ht @ w_up[e])
        out = out + gates[:, e:e + 1] * (hid @ w_down[e])
    return out.reshape(b, s, d)


def setup_inputs(seed: int = 0) -> dict:
    key = jax.random.key(seed)
    ks = jax.random.split(key, 20)

    def nrm(k, shape, scale):
        return jax.random.normal(k, shape, jnp.float32) * scale

    d = D_MODEL
    return {
        'x': nrm(ks[0], (BATCH, SEQ, d), 1.0),
        'c': nrm(ks[1], (BATCH, d), 1.0),
        'rel_bias': nrm(ks[2], (REL_BUCKETS, N_HEADS), 0.5),
        'w_ada': nrm(ks[3], (DEPTH, d, 6 * d), 0.5 * d ** -0.5),
        'b_ada': nrm(ks[4], (DEPTH, 6 * d), 0.1),
        'norm1_g': 1.0 + nrm(ks[5], (DEPTH, d), 0.02),
        'norm2_g': 1.0 + nrm(ks[6], (DEPTH, d), 0.02),
        'w_qkv': nrm(ks[7], (DEPTH, d, D_QKV), d ** -0.5),
        'attn_sinks': nrm(ks[8], (DEPTH, SWA_Q_HEADS), 0.5),
        'swa_out_g': 1.0 + nrm(ks[9], (DEPTH, D_SWA), 0.02),
        'moba_out_g': 1.0 + nrm(ks[10], (DEPTH, D_MOBA), 0.02),
        'w_o': nrm(ks[11], (DEPTH, D_MIX, d), D_MIX ** -0.5),
        'w_group': nrm(ks[12], (DEPTH, d, N_GROUPS), d ** -0.5),
        'b_group': nrm(ks[13], (DEPTH, N_GROUPS), 0.01),
        'w_expert_router': nrm(ks[14], (DEPTH, d, N_EXPERTS), d ** -0.5),
        'b_expert_router': nrm(ks[15], (DEPTH, N_EXPERTS), 0.01),
        'w_gate': nrm(ks[16], (DEPTH, N_EXPERTS, d, D_EXPERT), d ** -0.5),
        'w_up': nrm(ks[17], (DEPTH, N_EXPERTS, d, D_EXPERT), d ** -0.5),
        'w_down': nrm(ks[18], (DEPTH, N_EXPERTS, D_EXPERT, d), D_EXPERT ** -0.5),
        'final_g': 1.0 + nrm(ks[19], (d,), 0.02),
    }


def reference(x, c, rel_bias, w_ada, b_ada, norm1_g, norm2_g, w_qkv, attn_sinks, swa_out_g,
              moba_out_g, w_o, w_group, b_group, w_expert_router, b_expert_router,
              w_gate, w_up, w_down, final_g):
    b, s, _ = x.shape
    c_act = jax.nn.silu(c)
    bias_swa = rel_bias[:, :SWA_Q_HEADS]
    bias_moba = rel_bias[:, SWA_Q_HEADS:]
    for l in range(DEPTH):
        mod = c_act @ w_ada[l] + b_ada[l]
        sh1, sc1, g1, sh2, sc2, g2 = [m[:, None, :] for m in jnp.split(mod, 6, axis=-1)]
        h = rmsnorm(x, norm1_g[l]) * (1 + sc1) + sh1
        qkv = h @ w_qkv[l]
        q_a, k_a, v_a, q_b, k_b, v_b = jnp.split(qkv, SPLIT_POINTS, axis=-1)
        y_a = sliding_window_attention(
            q_a.reshape(b, s, SWA_Q_HEADS, HEAD_DIM),
            k_a.reshape(b, s, SWA_KV_HEADS, HEAD_DIM),
            v_a.reshape(b, s, SWA_KV_HEADS, HEAD_DIM),
            attn_sinks[l], bias_swa)
        y_b = moba_attention(
            q_b.reshape(b, s, MOBA_HEADS, HEAD_DIM),
            k_b.reshape(b, s, MOBA_HEADS, HEAD_DIM),
            v_b.reshape(b, s, MOBA_HEADS, HEAD_DIM),
            bias_moba)
        y = jnp.concatenate([rmsnorm(y_a, swa_out_g[l]), rmsnorm(y_b, moba_out_g[l])], axis=-1) @ w_o[l]
        x = x + g1 * y
        h = rmsnorm(x, norm2_g[l]) * (1 + sc2) + sh2
        x = x + g2 * hier_moe(h, w_group[l], b_group[l], w_expert_router[l], b_expert_router[l],
                              w_gate[l], w_up[l], w_down[l])
    return rmsnorm(x, final_g)
```

```python
import functools
import math

import jax
import jax.numpy as jnp
from jax import lax
from jax.experimental import pallas as pl
from jax.experimental.pallas import tpu as pltpu

F32 = jnp.float32
BF16 = jnp.bfloat16
I32 = jnp.int32
HIGHEST = lax.Precision.HIGHEST

D_MODEL = 2048
HEAD_DIM = 64
SWA_Q_HEADS = 16
SWA_KV_HEADS = 2
SWA_GROUP = SWA_Q_HEADS // SWA_KV_HEADS
SWA_WINDOW = 128
MOBA_HEADS = 16
MOBA_BLOCK = 256
MOBA_TOPK = 3
D_SWA = SWA_Q_HEADS * HEAD_DIM
D_SWA_KV = SWA_KV_HEADS * HEAD_DIM
D_MOBA = MOBA_HEADS * HEAD_DIM
D_QKV = D_SWA + 2 * D_SWA_KV + 3 * D_MOBA
REL_BUCKETS = 32
REL_MAX_DISTANCE = 2048
N_GROUPS = 4
EXPERTS_PER_GROUP = 8
N_EXPERTS = N_GROUPS * EXPERTS_PER_GROUP
D_EXPERT = 512
EPS = 1e-6
NEG = -1e30
SCALE = 1.0 / math.sqrt(HEAD_DIM)

LANES = 128
FAR_TILE = REL_MAX_DISTANCE // MOBA_BLOCK + 1
QKV_COL_TILE = D_QKV // 2
QKV_ROW_TILE = 512
OPROJ_ROW_TILE = 256
ROUTE_ROW_TILE = 512
MOVE_ROW_TILE = 256
EXPERT_ROW_TILE = 256
ADA_COL_TILE = 1024
VMEM_LIMIT = 56 * 1024 * 1024


def _cparams(sem, vmem=VMEM_LIMIT):
    return pltpu.CompilerParams(dimension_semantics=sem, vmem_limit_bytes=vmem)


def _rel_bucket(dist):
    n = jnp.maximum(dist, 0)
    max_exact = REL_BUCKETS // 2
    nf = jnp.maximum(n, 1).astype(F32)
    large = max_exact + (jnp.log(nf / max_exact) / math.log(REL_MAX_DISTANCE / max_exact)
                         * (REL_BUCKETS - max_exact)).astype(I32)
    large = jnp.minimum(large, REL_BUCKETS - 1)
    return jnp.where(n < max_exact, n, large)


def _silu(v):
    return v * (1.0 / (1.0 + jnp.exp(-v)))


def _ada_kernel(c_ref, w_ref, b_ref, o_ref):
    ca = _silu(c_ref[...])
    o_ref[0] = jnp.dot(ca, w_ref[0], precision=HIGHEST, preferred_element_type=F32) + b_ref[0]


def _ada_mod(c_pad, w_ada, b_ada):
    depth, d, n = w_ada.shape
    rows = c_pad.shape[0]
    return pl.pallas_call(
        _ada_kernel,
        out_shape=jax.ShapeDtypeStruct((depth, rows, n), F32),
        grid=(depth, n // ADA_COL_TILE),
        in_specs=[pl.BlockSpec((rows, d), lambda l, j: (0, 0)),
                  pl.BlockSpec((1, d, ADA_COL_TILE), lambda l, j: (l, 0, j)),
                  pl.BlockSpec((1, 1, ADA_COL_TILE), lambda l, j: (l, 0, j))],
        out_specs=pl.BlockSpec((1, rows, ADA_COL_TILE), lambda l, j: (l, 0, j)),
        compiler_params=_cparams(("arbitrary", "arbitrary")),
        name="ada_mod",
    )(c_pad, w_ada, b_ada.reshape(depth, 1, n))


def _bias_kernel(rb_ref, bkt_ref, o_ref, *, head0):
    h = pl.program_id(0) + head0
    b = bkt_ref[0]
    acc = jnp.zeros(b.shape, F32)
    for k in range(REL_BUCKETS):
        acc = jnp.where(b == k, rb_ref[k, h], acc)
    o_ref[0, 0] = acc


def _bias_table(rel_bias, bkt, n_heads, head0):
    nt, r, c = bkt.shape
    return pl.pallas_call(
        functools.partial(_bias_kernel, head0=head0),
        out_shape=jax.ShapeDtypeStruct((n_heads, nt, r, c), F32),
        grid=(n_heads, nt),
        in_specs=[pl.BlockSpec(memory_space=pltpu.SMEM),
                  pl.BlockSpec((1, r, c), lambda h, t: (t, 0, 0))],
        out_specs=pl.BlockSpec((1, 1, r, c), lambda h, t: (h, t, 0, 0)),
        compiler_params=_cparams(("arbitrary", "arbitrary")),
        name="bias_table",
    )(rel_bias, bkt)


def _rms(v):
    return lax.rsqrt(jnp.mean(v * v, axis=-1, keepdims=True) + EPS)


def _qkv_kernel(x_ref, sh_ref, sc_ref, g_ref, w_ref, o_ref):
    x = x_ref[...]
    h = (x * _rms(x)) * g_ref[...] * (1.0 + sc_ref[0]) + sh_ref[0]
    o_ref[...] = jnp.dot(h.astype(BF16), w_ref[...], preferred_element_type=F32).astype(BF16)


def _qkv_proj(x2, sh, sc, g, w_bf, seq):
    t, d = x2.shape
    tm = QKV_ROW_TILE
    per_b = seq // tm
    return pl.pallas_call(
        _qkv_kernel,
        out_shape=jax.ShapeDtypeStruct((t, D_QKV), BF16),
        grid=(D_QKV // QKV_COL_TILE, t // tm),
        in_specs=[pl.BlockSpec((tm, d), lambda j, i: (i, 0)),
                  pl.BlockSpec((1, 1, d), lambda j, i: (i // per_b, 0, 0)),
                  pl.BlockSpec((1, 1, d), lambda j, i: (i // per_b, 0, 0)),
                  pl.BlockSpec((1, d), lambda j, i: (0, 0)),
                  pl.BlockSpec((d, QKV_COL_TILE), lambda j, i: (0, j))],
        out_specs=pl.BlockSpec((tm, QKV_COL_TILE), lambda j, i: (i, j)),
        compiler_params=_cparams(("arbitrary", "arbitrary")),
        name="qkv_proj",
    )(x2, sh, sc, g, w_bf)


def _swa_kernel(sink_ref, q_ref, kvc_ref, kvp_ref, bias_ref, g_ref, o_ref):
    n = pl.program_id(1)
    w = SWA_WINDOW
    q = q_ref[0]
    kv = jnp.concatenate([kvp_ref[0], kvc_ref[0]], axis=0)
    qi = lax.broadcasted_iota(I32, (w, 2 * w), 0)
    kj = lax.broadcasted_iota(I32, (w, 2 * w), 1)
    dist = w + qi - kj
    band = jnp.where(dist >= 0, jnp.where(dist < SWA_WINDOW, 1, 0), 0)
    start_ok = jnp.where(kj >= w, 1, jnp.where(n > 0, 1, 0))
    valid = (band * start_ok) > 0
    outs = []
    for kh in range(SWA_KV_HEADS):
        k = kv[:, kh * HEAD_DIM:(kh + 1) * HEAD_DIM]
        v = kv[:, D_SWA_KV + kh * HEAD_DIM:D_SWA_KV + (kh + 1) * HEAD_DIM]
        for g in range(SWA_GROUP):
            h = kh * SWA_GROUP + g
            qh = q[:, h * HEAD_DIM:(h + 1) * HEAD_DIM]
            s = lax.dot_general(qh, k, (((1,), (1,)), ((), ())), preferred_element_type=F32)
            logits = jnp.where(valid, s * SCALE + bias_ref[h, 0], NEG)
            sink = sink_ref[h]
            m = jnp.maximum(jnp.max(logits, axis=-1, keepdims=True), sink)
            e = jnp.exp(logits - m)
            denom = jnp.sum(e, axis=-1, keepdims=True) + jnp.exp(sink - m)
            pv = jnp.dot(e.astype(BF16), v, preferred_element_type=F32)
            outs.append(pv / denom)
    y = jnp.concatenate(outs, axis=-1)
    o_ref[0] = ((y * _rms(y)) * g_ref[...]).astype(BF16)


def _swa_attention(qkv3, sinks, bias_tab, out_g):
    b, s, _ = qkv3.shape
    w = SWA_WINDOW
    kv_blk = D_SWA // (2 * D_SWA_KV)
    return pl.pallas_call(
        _swa_kernel,
        out_shape=jax.ShapeDtypeStruct((b, s, D_SWA), BF16),
        grid=(b, s // w),
        in_specs=[pl.BlockSpec(memory_space=pltpu.SMEM),
                  pl.BlockSpec((1, w, D_SWA), lambda bi, n: (bi, n, 0)),
                  pl.BlockSpec((1, w, 2 * D_SWA_KV), lambda bi, n: (bi, n, kv_blk)),
                  pl.BlockSpec((1, w, 2 * D_SWA_KV), lambda bi, n: (bi, jnp.maximum(n - 1, 0), kv_blk)),
                  pl.BlockSpec((SWA_Q_HEADS, 1, w, 2 * w), lambda bi, n: (0, 0, 0, 0)),
                  pl.BlockSpec((1, D_SWA), lambda bi, n: (0, 0))],
        out_specs=pl.BlockSpec((1, w, D_SWA), lambda bi, n: (bi, n, 0)),
        compiler_params=_cparams(("arbitrary", "arbitrary")),
        name="swa_attention",
    )(sinks, qkv3, qkv3, qkv3, bias_tab, out_g)


def _moba_kernel(q_ref, k_ref, v_ref, bias_ref, o_ref,
                 vt_sc, kmean_sc, rhs_sc, sel_sc, m_sc, l_sc, acc_sc, *, nblk):
    c = pl.program_id(2)
    blk = MOBA_BLOCK
    far = bias_ref.shape[1] - 1

    @pl.when(c == 0)
    def _prep():
        def body(j, carry):
            off = pl.multiple_of(j * blk, blk)
            vt_sc[j] = v_ref[0, pl.ds(off, blk), :].astype(F32).T.astype(BF16)
            kb = k_ref[0, pl.ds(off, blk), :].astype(F32)
            kmean_sc[pl.ds(j, 1), :] = jnp.sum(kb, axis=0, keepdims=True) * (1.0 / blk)
            return carry
        lax.fori_loop(0, nblk, body, 0)

    qt = (q_ref[0].astype(F32) * SCALE).T
    row = lax.broadcasted_iota(I32, (2 * HEAD_DIM, blk), 0)
    blk_i = lax.broadcasted_iota(I32, (nblk, blk), 0)
    kpos = lax.broadcasted_iota(I32, (blk, blk), 0)
    qpos = lax.broadcasted_iota(I32, (blk, blk), 1)
    own_off = pl.multiple_of(c * blk, blk)
    k_own = k_ref[0, pl.ds(own_off, blk), :]
    for hh in range(2):
        in_head = jnp.where(row >= hh * HEAD_DIM, jnp.where(row < (hh + 1) * HEAD_DIM, 1, 0), 0) > 0
        qt_h = jnp.where(in_head, qt, 0.0)
        rhs = qt_h.astype(BF16)
        rhs_sc[hh] = rhs
        gate = jnp.dot(kmean_sc[...], qt_h, precision=HIGHEST, preferred_element_type=F32)
        gate = jnp.where(blk_i < c, gate, NEG)
        sel = jnp.zeros((nblk, blk), F32)
        for _ in range(min(MOBA_TOPK, nblk)):
            mx = jnp.max(gate, axis=0, keepdims=True)
            idx = jnp.min(jnp.where(gate == mx, blk_i, nblk), axis=0, keepdims=True)
            hit = blk_i == idx
            sel = jnp.where(hit, jnp.where(blk_i < c, 1.0, 0.0), sel)
            gate = jnp.where(hit, -jnp.inf, gate)
        sel_sc[hh] = sel
        st = jnp.dot(k_own, rhs, preferred_element_type=F32) + bias_ref[hh, 0]
        st = jnp.where(qpos >= kpos, st, NEG)
        m = jnp.max(st, axis=0, keepdims=True)
        p = jnp.exp(st - m)
        m_sc[hh] = m
        l_sc[hh] = jnp.sum(p, axis=0, keepdims=True)
        acc_sc[hh] = jnp.dot(vt_sc[c, pl.ds(hh * HEAD_DIM, HEAD_DIM), :], p.astype(BF16),
                             preferred_element_type=F32)

    def past(j, carry):
        off = pl.multiple_of(j * blk, blk)
        kj = k_ref[0, pl.ds(off, blk), :]
        d = jnp.minimum(c - j, far)
        for hh in range(2):
            st = jnp.dot(kj, rhs_sc[hh], preferred_element_type=F32) + bias_ref[hh, d]
            chosen = sel_sc[hh, pl.ds(j, 1), :] > 0.5
            m_old = m_sc[hh]
            m_big = jnp.maximum(m_old, jnp.max(st, axis=0, keepdims=True))
            m_new = jnp.where(chosen, m_big, m_old)
            p = jnp.exp(st - m_big)
            alpha = jnp.exp(m_old - m_new)
            l_sc[hh] = alpha * l_sc[hh] + jnp.where(chosen, jnp.sum(p, axis=0, keepdims=True), 0.0)
            pv = jnp.dot(vt_sc[j, pl.ds(hh * HEAD_DIM, HEAD_DIM), :], p.astype(BF16),
                         preferred_element_type=F32)
            acc_sc[hh] = alpha * acc_sc[hh] + jnp.where(chosen, pv, 0.0)
            m_sc[hh] = m_new
        return carry
    lax.fori_loop(0, c, past, 0)

    ot = jnp.concatenate([acc_sc[0] / l_sc[0], acc_sc[1] / l_sc[1]], axis=0)
    o_ref[0] = ot.T


def _moba_attention(qkv3, bias_tab):
    b, s, _ = qkv3.shape
    blk = MOBA_BLOCK
    nblk = s // blk
    pairs = MOBA_HEADS // 2
    pw = 2 * HEAD_DIM
    q0 = (D_SWA + 2 * D_SWA_KV) // pw
    k0 = q0 + D_MOBA // pw
    v0 = k0 + D_MOBA // pw
    nt = bias_tab.shape[1]
    return pl.pallas_call(
        functools.partial(_moba_kernel, nblk=nblk),
        out_shape=jax.ShapeDtypeStruct((b, s, D_MOBA), F32),
        grid=(b, pairs, nblk),
        in_specs=[pl.BlockSpec((1, blk, pw), lambda bi, hp, c: (bi, c, q0 + hp)),
                  pl.BlockSpec((1, s, pw), lambda bi, hp, c: (bi, 0, k0 + hp)),
                  pl.BlockSpec((1, s, pw), lambda bi, hp, c: (bi, 0, v0 + hp)),
                  pl.BlockSpec((2, nt, blk, blk), lambda bi, hp, c: (hp, 0, 0, 0))],
        out_specs=pl.BlockSpec((1, blk, pw), lambda bi, hp, c: (bi, c, hp)),
        scratch_shapes=[pltpu.VMEM((nblk, pw, blk), BF16),
                        pltpu.VMEM((nblk, pw), F32),
                        pltpu.VMEM((2, pw, blk), BF16),
                        pltpu.VMEM((2, nblk, blk), F32),
                        pltpu.VMEM((2, 1, blk), F32),
                        pltpu.VMEM((2, 1, blk), F32),
                        pltpu.VMEM((2, HEAD_DIM, blk), F32)],
        compiler_params=_cparams(("arbitrary", "arbitrary", "arbitrary")),
        name="moba_attention",
    )(qkv3, qkv3, qkv3, bias_tab)


def _oproj_kernel(ya_ref, yb_ref, x_ref, gb_ref, wo_ref, g1_ref, n2g_ref, sc2_ref, sh2_ref,
                  wr_ref, br_ref, xo_ref, h2_ref, lg_ref):
    yb = yb_ref[...]
    ybn = ((yb * _rms(yb)) * gb_ref[...]).astype(BF16)
    y = (jnp.dot(ya_ref[...], wo_ref[0:D_SWA, :], preferred_element_type=F32)
         + jnp.dot(ybn, wo_ref[D_SWA:D_SWA + D_MOBA, :], preferred_element_type=F32))
    xn = x_ref[...] + g1_ref[0] * y
    xo_ref[...] = xn
    h2 = (xn * _rms(xn)) * n2g_ref[...] * (1.0 + sc2_ref[0]) + sh2_ref[0]
    h2_ref[...] = h2
    lg_ref[...] = jnp.dot(h2, wr_ref[...], precision=HIGHEST, preferred_element_type=F32) + br_ref[...]


def _oproj(ya, yb, x2, gb, wo_bf, g1, n2g, sc2, sh2, wr, br, seq):
    t, d = x2.shape
    tm = OPROJ_ROW_TILE
    per_b = seq // tm
    row = lambda i: (i, 0)
    fix = lambda i: (0, 0)
    bat = lambda i: (i // per_b, 0, 0)
    return pl.pallas_call(
        _oproj_kernel,
        out_shape=(jax.ShapeDtypeStruct((t, d), F32), jax.ShapeDtypeStruct((t, d), F32),
                   jax.ShapeDtypeStruct((t, LANES), F32)),
        grid=(t // tm,),
        in_specs=[pl.BlockSpec((tm, D_SWA), row), pl.BlockSpec((tm, D_MOBA), row),
                  pl.BlockSpec((tm, d), row), pl.BlockSpec((1, D_MOBA), fix),
                  pl.BlockSpec((D_SWA + D_MOBA, d), fix), pl.BlockSpec((1, 1, d), bat),
                  pl.BlockSpec((1, d), fix), pl.BlockSpec((1, 1, d), bat), pl.BlockSpec((1, 1, d), bat),
                  pl.BlockSpec((d, LANES), fix), pl.BlockSpec((1, LANES), fix)],
        out_specs=(pl.BlockSpec((tm, d), row), pl.BlockSpec((tm, d), row), pl.BlockSpec((tm, LANES), row)),
        compiler_params=_cparams(("arbitrary",)),
        name="oproj",
    )(ya, yb, x2, gb, wo_bf, g1, n2g, sc2, sh2, wr, br)


def _route_kernel(lg_ref, ri_ref, rf_ref, cnt_ref, carry_sc):
    i = pl.program_id(0)

    @pl.when(i == 0)
    def _init():
        carry_sc[...] = jnp.zeros_like(carry_sc)

    lg = lg_ref[...]
    tm = lg.shape[0]
    lane = lax.broadcasted_iota(I32, (tm, LANES), 1)
    ninf = -jnp.inf
    gl = jnp.where(lane < N_GROUPS, lg, ninf)
    ge = jnp.exp(gl - jnp.max(gl, axis=-1, keepdims=True))
    gp = ge / jnp.sum(ge, axis=-1, keepdims=True)
    p_grp = jnp.max(gp, axis=-1, keepdims=True)
    g_idx = jnp.min(jnp.where(gp == p_grp, lane, LANES), axis=-1, keepdims=True)
    lo = N_GROUPS + g_idx * EXPERTS_PER_GROUP
    in_grp = jnp.where(lane >= lo, jnp.where(lane < lo + EXPERTS_PER_GROUP, 1, 0), 0) > 0
    el = jnp.where(in_grp, lg, ninf)
    v0 = jnp.max(el, axis=-1, keepdims=True)
    i0 = jnp.min(jnp.where(el == v0, lane, LANES), axis=-1, keepdims=True)
    el = jnp.where(lane == i0, ninf, el)
    v1 = jnp.max(el, axis=-1, keepdims=True)
    i1 = jnp.min(jnp.where(el == v1, lane, LANES), axis=-1, keepdims=True)
    e0 = i0 - N_GROUPS
    e1 = i1 - N_GROUPS
    ex1 = jnp.exp(v1 - v0)
    den = 1.0 + ex1
    w0 = p_grp * (1.0 / den)
    w1 = p_grp * (ex1 / den)
    onehot = jnp.where(lane == e0, 1.0, jnp.where(lane == e1, 1.0, 0.0))
    r_i = lax.broadcasted_iota(I32, (tm, tm), 0)
    c_i = lax.broadcasted_iota(I32, (tm, tm), 1)
    tri = jnp.where(r_i > c_i, 1.0, 0.0).astype(BF16)
    cum = jnp.dot(tri, onehot.astype(BF16), preferred_element_type=F32) + carry_sc[...]
    pos0 = jnp.sum(jnp.where(lane == e0, cum, 0.0), axis=-1, keepdims=True).astype(I32)
    pos1 = jnp.sum(jnp.where(lane == e1, cum, 0.0), axis=-1, keepdims=True).astype(I32)
    carry_sc[...] = carry_sc[...] + jnp.sum(onehot, axis=0, keepdims=True)
    cnt_ref[...] = carry_sc[...]
    ri_ref[...] = jnp.where(lane == 0, e0, jnp.where(lane == 1, e1,
                            jnp.where(lane == 2, pos0, jnp.where(lane == 3, pos1, 0))))
    rf_ref[...] = jnp.where(lane == 0, w0, jnp.where(lane == 1, w1, 0.0))


def _route(logits):
    t = logits.shape[0]
    tm = ROUTE_ROW_TILE
    return pl.pallas_call(
        _route_kernel,
        out_shape=(jax.ShapeDtypeStruct((t, LANES), I32), jax.ShapeDtypeStruct((t, LANES), F32),
                   jax.ShapeDtypeStruct((1, LANES), F32)),
        grid=(t // tm,),
        in_specs=[pl.BlockSpec((tm, LANES), lambda i: (i, 0))],
        out_specs=(pl.BlockSpec((tm, LANES), lambda i: (i, 0)), pl.BlockSpec((tm, LANES), lambda i: (i, 0)),
                   pl.BlockSpec((1, LANES), lambda i: (0, 0))),
        scratch_shapes=[pltpu.VMEM((1, LANES), F32)],
        compiler_params=_cparams(("arbitrary",)),
        name="route",
    )(logits)


def _row_copy(src, dst, sem):
    return pltpu.make_async_copy(src, dst, sem)


def _dispatch_kernel(d0_ref, d1_ref, h_ref, xin_ref, xg_ref, sem):
    del xin_ref
    tm = h_ref.shape[0]
    base = pl.program_id(0) * tm

    def issue(r, carry):
        src = h_ref.at[pl.ds(r, 1), :]
        _row_copy(src, xg_ref.at[pl.ds(d0_ref[base + r], 1), :], sem.at[0]).start()
        _row_copy(src, xg_ref.at[pl.ds(d1_ref[base + r], 1), :], sem.at[0]).start()
        return carry
    lax.fori_loop(0, tm, issue, 0)

    def drain(r, carry):
        src = h_ref.at[pl.ds(r, 1), :]
        _row_copy(src, xg_ref.at[pl.ds(d0_ref[base + r], 1), :], sem.at[0]).wait()
        _row_copy(src, xg_ref.at[pl.ds(d1_ref[base + r], 1), :], sem.at[0]).wait()
        return carry
    lax.fori_loop(0, tm, drain, 0)


def _dispatch(d0, d1, h2, xg):
    t, d = h2.shape
    tm = MOVE_ROW_TILE
    return pl.pallas_call(
        _dispatch_kernel,
        out_shape=jax.ShapeDtypeStruct(xg.shape, xg.dtype),
        grid_spec=pltpu.PrefetchScalarGridSpec(
            num_scalar_prefetch=2, grid=(t // tm,),
            in_specs=[pl.BlockSpec((tm, d), lambda i, a, b: (i, 0)),
                      pl.BlockSpec(memory_space=pl.ANY)],
            out_specs=pl.BlockSpec(memory_space=pl.ANY),
            scratch_shapes=[pltpu.SemaphoreType.DMA((1,))]),
        input_output_aliases={3: 0},
        compiler_params=_cparams(("arbitrary",)),
        name="dispatch",
    )(d0, d1, h2, xg)


def _ffn_kernel(te_ref, nu_ref, x_ref, wg_ref, wu_ref, wd_ref, y_ref):
    del te_ref
    i = pl.program_id(0)

    @pl.when(i < nu_ref[0])
    def _compute():
        xb = x_ref[...].astype(BF16)
        g = jnp.dot(xb, wg_ref[0], preferred_element_type=F32)
        u = jnp.dot(xb, wu_ref[0], preferred_element_type=F32)
        hid = (_silu(g) * u).astype(BF16)
        y_ref[...] = jnp.dot(hid, wd_ref[0], preferred_element_type=F32)

    @pl.when(i >= nu_ref[0])
    def _skip():
        y_ref[...] = jnp.zeros_like(y_ref)


def _expert_ffn(tile_expert, n_used, xg, wg, wu, wd, layer):
    npad, d = xg.shape
    tr = EXPERT_ROW_TILE
    e_base = layer * N_EXPERTS
    wmap = lambda i, te, nu: (e_base + te[i], 0, 0)
    return pl.pallas_call(
        _ffn_kernel,
        out_shape=jax.ShapeDtypeStruct((npad, d), F32),
        grid_spec=pltpu.PrefetchScalarGridSpec(
            num_scalar_prefetch=2, grid=(npad // tr,),
            in_specs=[pl.BlockSpec((tr, d), lambda i, te, nu: (i, 0)),
                      pl.BlockSpec((1, d, D_EXPERT), wmap),
                      pl.BlockSpec((1, d, D_EXPERT), wmap),
                      pl.BlockSpec((1, D_EXPERT, d), wmap)],
            out_specs=pl.BlockSpec((tr, d), lambda i, te, nu: (i, 0))),
        compiler_params=_cparams(("arbitrary",)),
        name="expert_ffn",
    )(tile_expert, n_used, xg, wg, wu, wd)


def _combine_kernel(d0_ref, d1_ref, x_ref, rf_ref, g2_ref, fg_ref, y_ref, o_ref, buf, sem, *, final):
    tm = x_ref.shape[0]
    base = pl.program_id(0) * tm

    def issue(r, carry):
        _row_copy(y_ref.at[pl.ds(d0_ref[base + r], 1), :], buf.at[0, pl.ds(r, 1), :], sem.at[0]).start()
        _row_copy(y_ref.at[pl.ds(d1_ref[base + r], 1), :], buf.at[1, pl.ds(r, 1), :], sem.at[0]).start()
        return carry
    lax.fori_loop(0, tm, issue, 0)

    def drain(r, carry):
        _row_copy(y_ref.at[pl.ds(d0_ref[base + r], 1), :], buf.at[0, pl.ds(r, 1), :], sem.at[0]).wait()
        _row_copy(y_ref.at[pl.ds(d1_ref[base + r], 1), :], buf.at[1, pl.ds(r, 1), :], sem.at[0]).wait()
        return carry
    lax.fori_loop(0, tm, drain, 0)

    w0 = rf_ref[:, 0:1]
    w1 = rf_ref[:, 1:2]
    xn = x_ref[...] + g2_ref[0] * (w0 * buf[0] + w1 * buf[1])
    if final:
        xn = (xn * _rms(xn)) * fg_ref[...]
    o_ref[...] = xn


def _combine(d0, d1, x2, rf, g2, final_g, y, seq, final):
    t, d = x2.shape
    tm = MOVE_ROW_TILE
    per_b = seq // tm
    return pl.pallas_call(
        functools.partial(_combine_kernel, final=final),
        out_shape=jax.ShapeDtypeStruct((t, d), F32),
        grid_spec=pltpu.PrefetchScalarGridSpec(
            num_scalar_prefetch=2, grid=(t // tm,),
            in_specs=[pl.BlockSpec((tm, d), lambda i, a, b: (i, 0)),
                      pl.BlockSpec((tm, LANES), lambda i, a, b: (i, 0)),
                      pl.BlockSpec((1, 1, d), lambda i, a, b: (i // per_b, 0, 0)),
                      pl.BlockSpec((1, d), lambda i, a, b: (0, 0)),
                      pl.BlockSpec(memory_space=pl.ANY)],
            out_specs=pl.BlockSpec((tm, d), lambda i, a, b: (i, 0)),
            scratch_shapes=[pltpu.VMEM((2, tm, d), F32), pltpu.SemaphoreType.DMA((1,))]),
        compiler_params=_cparams(("arbitrary",)),
        name="combine",
    )(d0, d1, x2, rf, g2, final_g, y)


def _routing_plan(ri, cnt, n_tiles):
    tr = EXPERT_ROW_TILE
    counts = cnt[0, :N_EXPERTS].astype(I32)
    tiles_e = (counts + tr - 1) // tr
    tile_end = jnp.cumsum(tiles_e)
    row_off = (tile_end - tiles_e) * tr
    d0 = jnp.take(row_off, ri[:, 0]) + ri[:, 2]
    d1 = jnp.take(row_off, ri[:, 1]) + ri[:, 3]
    n_used = tile_end[-1]
    tid = jnp.minimum(jnp.arange(n_tiles, dtype=I32), n_used - 1)
    tile_expert = jnp.sum((tid[:, None] >= tile_end[None, :]).astype(I32), axis=1)
    return d0, d1, tile_expert.astype(I32), n_used.reshape(1).astype(I32)


def kernel(x, c, rel_bias, w_ada, b_ada, norm1_g, norm2_g, w_qkv, attn_sinks, swa_out_g, moba_out_g,
           w_o, w_group, b_group, w_expert_router, b_expert_router, w_gate, w_up, w_down, final_g):
    b, s, d = x.shape
    depth = w_ada.shape[0]
    t = b * s
    nblk = s // MOBA_BLOCK

    w = SWA_WINDOW
    swa_dist = (w + jnp.arange(w, dtype=I32))[:, None] - jnp.arange(2 * w, dtype=I32)[None, :]
    swa_bkt = _rel_bucket(swa_dist)[None]
    n_tab = min(nblk, FAR_TILE + 1)
    pos = jnp.arange(MOBA_BLOCK, dtype=I32)
    moba_dist = (jnp.arange(n_tab, dtype=I32)[:, None, None] * MOBA_BLOCK
                 + pos[None, None, :] - pos[None, :, None])
    moba_bkt = _rel_bucket(moba_dist)
    swa_tab = _bias_table(rel_bias, swa_bkt, SWA_Q_HEADS, 0)
    moba_tab = _bias_table(rel_bias, moba_bkt, MOBA_HEADS, SWA_Q_HEADS)

    c_pad = jnp.zeros((8, d), F32).at[:b].set(c)
    mod = _ada_mod(c_pad, w_ada, b_ada)[:, :b]

    w_qkv_bf = w_qkv.astype(BF16)
    w_o_bf = w_o.astype(BF16)
    wg_bf = w_gate.astype(BF16).reshape(depth * N_EXPERTS, d, D_EXPERT)
    wu_bf = w_up.astype(BF16).reshape(depth * N_EXPERTS, d, D_EXPERT)
    wd_bf = w_down.astype(BF16).reshape(depth * N_EXPERTS, D_EXPERT, d)

    n_tiles = (t * 2) // EXPERT_ROW_TILE + N_EXPERTS
    xg = jnp.zeros((n_tiles * EXPERT_ROW_TILE, d), F32)
    x2 = x.reshape(t, d)
    fg = final_g.reshape(1, d)
    for l in range(depth):
        sh1, sc1, g1, sh2, sc2, g2 = [mod[l, :, k * d:(k + 1) * d].reshape(b, 1, d) for k in range(6)]
        qkv = _qkv_proj(x2, sh1, sc1, norm1_g[l].reshape(1, d), w_qkv_bf[l], s)
        qkv3 = qkv.reshape(b, s, D_QKV)
        ya = _swa_attention(qkv3, attn_sinks[l], swa_tab, swa_out_g[l].reshape(1, D_SWA))
        yb = _moba_attention(qkv3, moba_tab)
        wr = jnp.zeros((d, LANES), F32).at[:, :N_GROUPS].set(w_group[l])
        wr = wr.at[:, N_GROUPS:N_GROUPS + N_EXPERTS].set(w_expert_router[l])
        br = jnp.zeros((1, LANES), F32).at[0, :N_GROUPS].set(b_group[l])
        br = br.at[0, N_GROUPS:N_GROUPS + N_EXPERTS].set(b_expert_router[l])
        x2, h2, logits = _oproj(ya.reshape(t, D_SWA), yb.reshape(t, D_MOBA), x2,
                                moba_out_g[l].reshape(1, D_MOBA), w_o_bf[l], g1,
                                norm2_g[l].reshape(1, d), sc2, sh2, wr, br, s)
        ri, rf, cnt = _route(logits)
        d0, d1, tile_expert, n_used = _routing_plan(ri, cnt, n_tiles)
        xg = _dispatch(d0, d1, h2, xg)
        y = _expert_ffn(tile_expert, n_used, xg, wg_bf, wu_bf, wd_bf, l)
        x2 = _combine(d0, d1, x2, rf, g2, fg, y, s, final=(l == depth - 1))
    return x2.reshape(b, s, d)
```

```python
import functools
import math

import jax
import jax.numpy as jnp
from jax import lax
from jax.experimental import pallas as pl
from jax.experimental.pallas import tpu as pltpu

F32 = jnp.float32
BF16 = jnp.bfloat16
I32 = jnp.int32
HIGHEST = lax.Precision.HIGHEST

D_MODEL = 2048
HEAD_DIM = 64
SWA_Q_HEADS = 16
SWA_KV_HEADS = 2
SWA_GROUP = SWA_Q_HEADS // SWA_KV_HEADS
SWA_WINDOW = 128
MOBA_HEADS = 16
MOBA_BLOCK = 256
MOBA_TOPK = 3
D_SWA = SWA_Q_HEADS * HEAD_DIM
D_SWA_KV = SWA_KV_HEADS * HEAD_DIM
D_MOBA = MOBA_HEADS * HEAD_DIM
D_QKV = D_SWA + 2 * D_SWA_KV + 3 * D_MOBA
REL_BUCKETS = 32
REL_MAX_DISTANCE = 2048
N_GROUPS = 4
EXPERTS_PER_GROUP = 8
N_EXPERTS = N_GROUPS * EXPERTS_PER_GROUP
D_EXPERT = 512
EPS = 1e-6
NEG = -1e30
SCALE = 1.0 / math.sqrt(HEAD_DIM)
LOG2E = math.log2(math.e)

LANES = 128
FAR_TILE = REL_MAX_DISTANCE // MOBA_BLOCK + 1
QKV_COL_TILE = D_QKV // 2
QKV_ROW_TILE = 512
OPROJ_ROW_TILE = 256
ROUTE_ROW_TILE = 512
MOVE_ROW_TILE = 256
EXPERT_ROW_TILE = 256
ADA_COL_TILE = 1024
DMA_UNROLL = 8
VMEM_LIMIT = 56 * 1024 * 1024


def _cparams(sem, vmem=VMEM_LIMIT):
    return pltpu.CompilerParams(dimension_semantics=sem, vmem_limit_bytes=vmem)


def _rel_bucket(dist):
    n = jnp.maximum(dist, 0)
    max_exact = REL_BUCKETS // 2
    nf = jnp.maximum(n, 1).astype(F32)
    large = max_exact + (jnp.log(nf / max_exact) / math.log(REL_MAX_DISTANCE / max_exact)
                         * (REL_BUCKETS - max_exact)).astype(I32)
    large = jnp.minimum(large, REL_BUCKETS - 1)
    return jnp.where(n < max_exact, n, large)


def _silu(v):
    return v * (1.0 / (1.0 + jnp.exp(-v)))


def _ada_kernel(c_ref, w_ref, b_ref, o_ref):
    ca = _silu(c_ref[...])
    o_ref[0] = jnp.dot(ca, w_ref[0], precision=HIGHEST, preferred_element_type=F32) + b_ref[0]


def _ada_mod(c_pad, w_ada, b_ada):
    depth, d, n = w_ada.shape
    rows = c_pad.shape[0]
    return pl.pallas_call(
        _ada_kernel,
        out_shape=jax.ShapeDtypeStruct((depth, rows, n), F32),
        grid=(depth, n // ADA_COL_TILE),
        in_specs=[pl.BlockSpec((rows, d), lambda l, j: (0, 0)),
                  pl.BlockSpec((1, d, ADA_COL_TILE), lambda l, j: (l, 0, j)),
                  pl.BlockSpec((1, 1, ADA_COL_TILE), lambda l, j: (l, 0, j))],
        out_specs=pl.BlockSpec((1, rows, ADA_COL_TILE), lambda l, j: (l, 0, j)),
        compiler_params=_cparams(("arbitrary", "arbitrary")),
        name="ada_mod",
    )(c_pad, w_ada, b_ada.reshape(depth, 1, n))


def _bias_kernel(rb_ref, bkt_ref, o_ref, *, head0, scale):
    h = pl.program_id(0) + head0
    b = bkt_ref[0]
    acc = jnp.full(b.shape, NEG, F32)
    for k in range(REL_BUCKETS):
        acc = jnp.where(b == k, rb_ref[k, h] * scale, acc)
    o_ref[0, 0] = acc


def _bias_table(rel_bias, bkt, n_heads, head0, scale=1.0):
    nt, r, c = bkt.shape
    return pl.pallas_call(
        functools.partial(_bias_kernel, head0=head0, scale=scale),
        out_shape=jax.ShapeDtypeStruct((n_heads, nt, r, c), F32),
        grid=(n_heads, nt),
        in_specs=[pl.BlockSpec(memory_space=pltpu.SMEM),
                  pl.BlockSpec((1, r, c), lambda h, t: (t, 0, 0))],
        out_specs=pl.BlockSpec((1, 1, r, c), lambda h, t: (h, t, 0, 0)),
        compiler_params=_cparams(("arbitrary", "arbitrary")),
        name="bias_table",
    )(rel_bias, bkt)


def _rms(v):
    return lax.rsqrt(jnp.mean(v * v, axis=-1, keepdims=True) + EPS)


def _qkv_kernel(x_ref, sh_ref, sc_ref, g_ref, w_ref, o_ref):
    x = x_ref[...]
    h = (x * _rms(x)) * g_ref[...] * (1.0 + sc_ref[0]) + sh_ref[0]
    o_ref[...] = jnp.dot(h.astype(BF16), w_ref[...], preferred_element_type=F32).astype(BF16)


def _qkv_proj(x2, sh, sc, g, w_bf, seq):
    t, d = x2.shape
    tm = QKV_ROW_TILE
    per_b = seq // tm
    return pl.pallas_call(
        _qkv_kernel,
        out_shape=jax.ShapeDtypeStruct((t, D_QKV), BF16),
        grid=(D_QKV // QKV_COL_TILE, t // tm),
        in_specs=[pl.BlockSpec((tm, d), lambda j, i: (i, 0)),
                  pl.BlockSpec((1, 1, d), lambda j, i: (i // per_b, 0, 0)),
                  pl.BlockSpec((1, 1, d), lambda j, i: (i // per_b, 0, 0)),
                  pl.BlockSpec((1, d), lambda j, i: (0, 0)),
                  pl.BlockSpec((d, QKV_COL_TILE), lambda j, i: (0, j))],
        out_specs=pl.BlockSpec((tm, QKV_COL_TILE), lambda j, i: (i, j)),
        compiler_params=_cparams(("arbitrary", "arbitrary")),
        name="qkv_proj",
    )(x2, sh, sc, g, w_bf)


def _swa_kernel(sink_ref, q_ref, kvc_ref, kvp_ref, bias_ref, g_ref, o_ref):
    n = pl.program_id(1)
    w = SWA_WINDOW
    q = q_ref[0]
    kv = jnp.concatenate([kvp_ref[0], kvc_ref[0]], axis=0)
    qi = lax.broadcasted_iota(I32, (w, 2 * w), 0)
    kj = lax.broadcasted_iota(I32, (w, 2 * w), 1)
    dist = w + qi - kj
    band = jnp.where(dist >= 0, jnp.where(dist < SWA_WINDOW, 1, 0), 0)
    start_ok = jnp.where(kj >= w, 1, jnp.where(n > 0, 1, 0))
    valid = (band * start_ok) > 0
    outs = []
    for kh in range(SWA_KV_HEADS):
        k = kv[:, kh * HEAD_DIM:(kh + 1) * HEAD_DIM]
        v = kv[:, D_SWA_KV + kh * HEAD_DIM:D_SWA_KV + (kh + 1) * HEAD_DIM]
        for g in range(SWA_GROUP):
            h = kh * SWA_GROUP + g
            qh = q[:, h * HEAD_DIM:(h + 1) * HEAD_DIM]
            s = lax.dot_general(qh, k, (((1,), (1,)), ((), ())), preferred_element_type=F32)
            logits = jnp.where(valid, s * SCALE + bias_ref[h, 0], NEG)
            sink = sink_ref[h]
            m = jnp.maximum(jnp.max(logits, axis=-1, keepdims=True), sink)
            e = jnp.exp(logits - m)
            denom = jnp.sum(e, axis=-1, keepdims=True) + jnp.exp(sink - m)
            pv = jnp.dot(e.astype(BF16), v, preferred_element_type=F32)
            outs.append(pv / denom)
    y = jnp.concatenate(outs, axis=-1)
    o_ref[0] = ((y * _rms(y)) * g_ref[...]).astype(BF16)


def _swa_attention(qkv3, sinks, bias_tab, out_g):
    b, s, _ = qkv3.shape
    w = SWA_WINDOW
    kv_blk = D_SWA // (2 * D_SWA_KV)
    return pl.pallas_call(
        _swa_kernel,
        out_shape=jax.ShapeDtypeStruct((b, s, D_SWA), BF16),
        grid=(b, s // w),
        in_specs=[pl.BlockSpec(memory_space=pltpu.SMEM),
                  pl.BlockSpec((1, w, D_SWA), lambda bi, n: (bi, n, 0)),
                  pl.BlockSpec((1, w, 2 * D_SWA_KV), lambda bi, n: (bi, n, kv_blk)),
                  pl.BlockSpec((1, w, 2 * D_SWA_KV), lambda bi, n: (bi, jnp.maximum(n - 1, 0), kv_blk)),
                  pl.BlockSpec((SWA_Q_HEADS, 1, w, 2 * w), lambda bi, n: (0, 0, 0, 0)),
                  pl.BlockSpec((1, D_SWA), lambda bi, n: (0, 0))],
        out_specs=pl.BlockSpec((1, w, D_SWA), lambda bi, n: (bi, n, 0)),
        compiler_params=_cparams(("arbitrary", "arbitrary")),
        name="swa_attention",
    )(sinks, qkv3, qkv3, qkv3, bias_tab, out_g)


def _moba_kernel(q_ref, k_ref, v_ref, bias_ref, o_ref,
                 vt_sc, kmean_sc, rhs_sc, sel_sc, s_sc, mb_sc, al_sc, m_sc, l_sc, acc_sc, *, nblk):
    c = pl.program_id(2)
    blk = MOBA_BLOCK
    far = bias_ref.shape[1] - 1

    @pl.when(c == 0)
    def _prep():
        def body(j, carry):
            off = pl.multiple_of(j * blk, blk)
            vt_sc[j] = v_ref[0, pl.ds(off, blk), :].astype(F32).T.astype(BF16)
            kb = k_ref[0, pl.ds(off, blk), :].astype(F32)
            kmean_sc[pl.ds(j, 1), :] = jnp.sum(kb, axis=0, keepdims=True) * (1.0 / blk)
            return carry
        lax.fori_loop(0, nblk, body, 0)

    qt = q_ref[0].astype(F32).T
    row = lax.broadcasted_iota(I32, (2 * HEAD_DIM, blk), 0)
    blk_i = lax.broadcasted_iota(I32, (nblk, blk), 0)
    for hh in range(2):
        in_head = jnp.where(row >= hh * HEAD_DIM, jnp.where(row < (hh + 1) * HEAD_DIM, 1, 0), 0) > 0
        qt_h = jnp.where(in_head, qt, 0.0)
        rhs_sc[hh] = (qt_h * (SCALE * LOG2E)).astype(BF16)
        gate = jnp.dot(kmean_sc[...], qt_h, precision=HIGHEST, preferred_element_type=F32)
        gate = jnp.where(blk_i < c, gate, NEG)
        sel = jnp.where(blk_i == c, 1.0, 0.0)
        for _ in range(min(MOBA_TOPK, nblk)):
            mx = jnp.max(gate, axis=0, keepdims=True)
            idx = jnp.min(jnp.where(gate == mx, blk_i, nblk), axis=0, keepdims=True)
            hit = blk_i == idx
            sel = jnp.where(hit, jnp.where(blk_i < c, 1.0, sel), sel)
            gate = jnp.where(hit, -jnp.inf, gate)
        sel_sc[hh] = sel
        m_sc[hh] = jnp.full((1, blk), NEG, F32)
        l_sc[hh] = jnp.zeros((1, blk), F32)
        acc_sc[hh] = jnp.zeros((HEAD_DIM, blk), F32)

    def stage_a(u):
        m = [m_sc[0], m_sc[1]]
        for ti in range(2):
            jt = 2 * u + ti
            off = pl.multiple_of(jt * blk, blk)
            kj = k_ref[0, pl.ds(off, blk), :]
            d = jnp.clip(c - jt, 0, far)
            for hh in range(2):
                st = jnp.dot(kj, rhs_sc[hh], preferred_element_type=F32) + bias_ref[hh, d]
                s_sc[ti, hh] = st
                chosen = sel_sc[hh, pl.ds(jt, 1), :] > 0.5
                m_big = jnp.maximum(m[hh], jnp.max(st, axis=0, keepdims=True))
                m_new = jnp.where(chosen, m_big, m[hh])
                mb_sc[ti, hh] = m_big
                al_sc[ti, hh] = jnp.exp2(m[hh] - m_new)
                m[hh] = m_new
        m_sc[0] = m[0]
        m_sc[1] = m[1]

    def stage_b(u):
        for ti in range(2):
            jt = 2 * u + ti
            for hh in range(2):
                p = jnp.exp2(s_sc[ti, hh] - mb_sc[ti, hh])
                chosen = sel_sc[hh, pl.ds(jt, 1), :] > 0.5
                alpha = al_sc[ti, hh]
                l_sc[hh] = alpha * l_sc[hh] + jnp.where(chosen, jnp.sum(p, axis=0, keepdims=True), 0.0)
                pv = jnp.dot(vt_sc[jt, pl.ds(hh * HEAD_DIM, HEAD_DIM), :], p.astype(BF16),
                             preferred_element_type=F32)
                acc_sc[hh] = alpha * acc_sc[hh] + jnp.where(chosen, pv, 0.0)

    n_pairs = (c + 2) // 2
    stage_a(0)

    def body(u, carry):
        stage_b(u - 1)
        stage_a(u)
        return carry
    lax.fori_loop(1, n_pairs, body, 0)
    stage_b(n_pairs - 1)

    ot = jnp.concatenate([acc_sc[0] / l_sc[0], acc_sc[1] / l_sc[1]], axis=0)
    o_ref[0] = ot.T


def _moba_attention(qkv3, bias_tab):
    b, s, _ = qkv3.shape
    blk = MOBA_BLOCK
    nblk = s // blk
    assert nblk % 2 == 0
    pairs = MOBA_HEADS // 2
    pw = 2 * HEAD_DIM
    q0 = (D_SWA + 2 * D_SWA_KV) // pw
    k0 = q0 + D_MOBA // pw
    v0 = k0 + D_MOBA // pw
    nt = bias_tab.shape[1]
    return pl.pallas_call(
        functools.partial(_moba_kernel, nblk=nblk),
        out_shape=jax.ShapeDtypeStruct((b, s, D_MOBA), F32),
        grid=(b, pairs, nblk),
        in_specs=[pl.BlockSpec((1, blk, pw), lambda bi, hp, c: (bi, c, q0 + hp)),
                  pl.BlockSpec((1, s, pw), lambda bi, hp, c: (bi, 0, k0 + hp)),
                  pl.BlockSpec((1, s, pw), lambda bi, hp, c: (bi, 0, v0 + hp)),
                  pl.BlockSpec((2, nt, blk, blk), lambda bi, hp, c: (hp, 0, 0, 0))],
        out_specs=pl.BlockSpec((1, blk, pw), lambda bi, hp, c: (bi, c, hp)),
        scratch_shapes=[pltpu.VMEM((nblk, pw, blk), BF16),
                        pltpu.VMEM((nblk, pw), F32),
                        pltpu.VMEM((2, pw, blk), BF16),
                        pltpu.VMEM((2, nblk, blk), F32),
                        pltpu.VMEM((2, 2, blk, blk), F32),
                        pltpu.VMEM((2, 2, 1, blk), F32),
                        pltpu.VMEM((2, 2, 1, blk), F32),
                        pltpu.VMEM((2, 1, blk), F32),
                        pltpu.VMEM((2, 1, blk), F32),
                        pltpu.VMEM((2, HEAD_DIM, blk), F32)],
        compiler_params=_cparams(("arbitrary", "arbitrary", "arbitrary")),
        name="moba_attention",
    )(qkv3, qkv3, qkv3, bias_tab)


def _oproj_kernel(ya_ref, yb_ref, x_ref, gb_ref, wo_ref, g1_ref, n2g_ref, sc2_ref, sh2_ref,
                  wrh_ref, wrl_ref, br_ref, xo_ref, h2_ref, lg_ref):
    yb = yb_ref[...]
    ybn = ((yb * _rms(yb)) * gb_ref[...]).astype(BF16)
    y = (jnp.dot(ya_ref[...], wo_ref[0:D_SWA, :], preferred_element_type=F32)
         + jnp.dot(ybn, wo_ref[D_SWA:D_SWA + D_MOBA, :], preferred_element_type=F32))
    xn = x_ref[...] + g1_ref[0] * y
    xo_ref[...] = xn
    h2 = (xn * _rms(xn)) * n2g_ref[...] * (1.0 + sc2_ref[0]) + sh2_ref[0]
    h2_ref[...] = h2
    hi = h2.astype(BF16)
    lo = (h2 - hi.astype(F32)).astype(BF16)
    lg_ref[...] = (jnp.dot(hi, wrh_ref[...], preferred_element_type=F32)
                   + jnp.dot(lo, wrh_ref[...], preferred_element_type=F32)
                   + jnp.dot(hi, wrl_ref[...], preferred_element_type=F32) + br_ref[...])


def _oproj(ya, yb, x2, gb, wo_bf, g1, n2g, sc2, sh2, wr_hi, wr_lo, br, seq):
    t, d = x2.shape
    tm = OPROJ_ROW_TILE
    per_b = seq // tm
    row = lambda i: (i, 0)
    fix = lambda i: (0, 0)
    bat = lambda i: (i // per_b, 0, 0)
    return pl.pallas_call(
        _oproj_kernel,
        out_shape=(jax.ShapeDtypeStruct((t, d), F32), jax.ShapeDtypeStruct((t, d), F32),
                   jax.ShapeDtypeStruct((t, LANES), F32)),
        grid=(t // tm,),
        in_specs=[pl.BlockSpec((tm, D_SWA), row), pl.BlockSpec((tm, D_MOBA), row),
                  pl.BlockSpec((tm, d), row), pl.BlockSpec((1, D_MOBA), fix),
                  pl.BlockSpec((D_SWA + D_MOBA, d), fix), pl.BlockSpec((1, 1, d), bat),
                  pl.BlockSpec((1, d), fix), pl.BlockSpec((1, 1, d), bat), pl.BlockSpec((1, 1, d), bat),
                  pl.BlockSpec((d, LANES), fix), pl.BlockSpec((d, LANES), fix), pl.BlockSpec((1, LANES), fix)],
        out_specs=(pl.BlockSpec((tm, d), row), pl.BlockSpec((tm, d), row), pl.BlockSpec((tm, LANES), row)),
        compiler_params=_cparams(("arbitrary",)),
        name="oproj",
    )(ya, yb, x2, gb, wo_bf, g1, n2g, sc2, sh2, wr_hi, wr_lo, br)


def _route_kernel(lg_ref, ri_ref, rf_ref, cnt_ref, carry_sc):
    i = pl.program_id(0)

    @pl.when(i == 0)
    def _init():
        carry_sc[...] = jnp.zeros_like(carry_sc)

    lg = lg_ref[...]
    tm = lg.shape[0]
    lane = lax.broadcasted_iota(I32, (tm, LANES), 1)
    ninf = -jnp.inf
    gl = jnp.where(lane < N_GROUPS, lg, ninf)
    ge = jnp.exp(gl - jnp.max(gl, axis=-1, keepdims=True))
    gp = ge / jnp.sum(ge, axis=-1, keepdims=True)
    p_grp = jnp.max(gp, axis=-1, keepdims=True)
    g_idx = jnp.min(jnp.where(gp == p_grp, lane, LANES), axis=-1, keepdims=True)
    lo = N_GROUPS + g_idx * EXPERTS_PER_GROUP
    in_grp = jnp.where(lane >= lo, jnp.where(lane < lo + EXPERTS_PER_GROUP, 1, 0), 0) > 0
    el = jnp.where(in_grp, lg, ninf)
    v0 = jnp.max(el, axis=-1, keepdims=True)
    i0 = jnp.min(jnp.where(el == v0, lane, LANES), axis=-1, keepdims=True)
    el = jnp.where(lane == i0, ninf, el)
    v1 = jnp.max(el, axis=-1, keepdims=True)
    i1 = jnp.min(jnp.where(el == v1, lane, LANES), axis=-1, keepdims=True)
    e0 = i0 - N_GROUPS
    e1 = i1 - N_GROUPS
    ex1 = jnp.exp(v1 - v0)
    den = 1.0 + ex1
    w0 = p_grp * (1.0 / den)
    w1 = p_grp * (ex1 / den)
    onehot = jnp.where(lane == e0, 1.0, jnp.where(lane == e1, 1.0, 0.0))
    r_i = lax.broadcasted_iota(I32, (tm, tm), 0)
    c_i = lax.broadcasted_iota(I32, (tm, tm), 1)
    tri = jnp.where(r_i > c_i, 1.0, 0.0).astype(BF16)
    cum = jnp.dot(tri, onehot.astype(BF16), preferred_element_type=F32) + carry_sc[...]
    pos0 = jnp.sum(jnp.where(lane == e0, cum, 0.0), axis=-1, keepdims=True).astype(I32)
    pos1 = jnp.sum(jnp.where(lane == e1, cum, 0.0), axis=-1, keepdims=True).astype(I32)
    carry_sc[...] = carry_sc[...] + jnp.sum(onehot, axis=0, keepdims=True)
    cnt_ref[...] = carry_sc[...]
    ri_ref[...] = jnp.where(lane == 0, e0, jnp.where(lane == 1, e1,
                            jnp.where(lane == 2, pos0, jnp.where(lane == 3, pos1, 0))))
    rf_ref[...] = jnp.where(lane == 0, w0, jnp.where(lane == 1, w1, 0.0))


def _route(logits):
    t = logits.shape[0]
    tm = ROUTE_ROW_TILE
    return pl.pallas_call(
        _route_kernel,
        out_shape=(jax.ShapeDtypeStruct((t, LANES), I32), jax.ShapeDtypeStruct((t, LANES), F32),
                   jax.ShapeDtypeStruct((1, LANES), F32)),
        grid=(t // tm,),
        in_specs=[pl.BlockSpec((tm, LANES), lambda i: (i, 0))],
        out_specs=(pl.BlockSpec((tm, LANES), lambda i: (i, 0)), pl.BlockSpec((tm, LANES), lambda i: (i, 0)),
                   pl.BlockSpec((1, LANES), lambda i: (0, 0))),
        scratch_shapes=[pltpu.VMEM((1, LANES), F32)],
        compiler_params=_cparams(("arbitrary",)),
        name="route",
    )(logits)


def _row_copy(src, dst, sem):
    return pltpu.make_async_copy(src, dst, sem)


def _dispatch_kernel(d0_ref, d1_ref, h_ref, xin_ref, xg_ref, sem):
    del xin_ref
    tm = h_ref.shape[0]
    base = pl.program_id(0) * tm

    def issue(r, carry):
        src = h_ref.at[pl.ds(r, 1), :]
        _row_copy(src, xg_ref.at[pl.ds(d0_ref[base + r], 1), :], sem.at[0]).start()
        _row_copy(src, xg_ref.at[pl.ds(d1_ref[base + r], 1), :], sem.at[0]).start()
        return carry
    lax.fori_loop(0, tm, issue, 0, unroll=DMA_UNROLL)

    def drain(r, carry):
        src = h_ref.at[pl.ds(r, 1), :]
        _row_copy(src, xg_ref.at[pl.ds(d0_ref[base + r], 1), :], sem.at[0]).wait()
        _row_copy(src, xg_ref.at[pl.ds(d1_ref[base + r], 1), :], sem.at[0]).wait()
        return carry
    lax.fori_loop(0, tm, drain, 0, unroll=DMA_UNROLL)


def _dispatch(d0, d1, h2, xg):
    t, d = h2.shape
    tm = MOVE_ROW_TILE
    return pl.pallas_call(
        _dispatch_kernel,
        out_shape=jax.ShapeDtypeStruct(xg.shape, xg.dtype),
        grid_spec=pltpu.PrefetchScalarGridSpec(
            num_scalar_prefetch=2, grid=(t // tm,),
            in_specs=[pl.BlockSpec((tm, d), lambda i, a, b: (i, 0)),
                      pl.BlockSpec(memory_space=pl.ANY)],
            out_specs=pl.BlockSpec(memory_space=pl.ANY),
            scratch_shapes=[pltpu.SemaphoreType.DMA((1,))]),
        input_output_aliases={3: 0},
        compiler_params=_cparams(("arbitrary",)),
        name="dispatch",
    )(d0, d1, h2, xg)


def _ffn_kernel(te_ref, nu_ref, x_ref, wg_ref, wu_ref, wd_ref, y_ref, wg_sc, wu_sc, wd_sc):
    i = pl.program_id(0)
    new_expert = jnp.logical_or(i == 0, te_ref[i] != te_ref[jnp.maximum(i - 1, 0)])

    @pl.when(new_expert)
    def _cast():
        wg_sc[...] = wg_ref[0].astype(BF16)
        wu_sc[...] = wu_ref[0].astype(BF16)
        wd_sc[...] = wd_ref[0].astype(BF16)

    @pl.when(i < nu_ref[0])
    def _compute():
        xb = x_ref[...].astype(BF16)
        g = jnp.dot(xb, wg_sc[...], preferred_element_type=F32)
        u = jnp.dot(xb, wu_sc[...], preferred_element_type=F32)
        hid = (_silu(g) * u).astype(BF16)
        y_ref[...] = jnp.dot(hid, wd_sc[...], preferred_element_type=F32)

    @pl.when(i >= nu_ref[0])
    def _skip():
        y_ref[...] = jnp.zeros_like(y_ref)


def _expert_ffn(tile_expert, n_used, xg, wg, wu, wd, layer):
    npad, d = xg.shape
    tr = EXPERT_ROW_TILE
    e_base = layer * N_EXPERTS
    wmap = lambda i, te, nu: (e_base + te[i], 0, 0)
    return pl.pallas_call(
        _ffn_kernel,
        out_shape=jax.ShapeDtypeStruct((npad, d), F32),
        grid_spec=pltpu.PrefetchScalarGridSpec(
            num_scalar_prefetch=2, grid=(npad // tr,),
            in_specs=[pl.BlockSpec((tr, d), lambda i, te, nu: (i, 0)),
                      pl.BlockSpec((1, d, D_EXPERT), wmap),
                      pl.BlockSpec((1, d, D_EXPERT), wmap),
                      pl.BlockSpec((1, D_EXPERT, d), wmap)],
            out_specs=pl.BlockSpec((tr, d), lambda i, te, nu: (i, 0)),
            scratch_shapes=[pltpu.VMEM((d, D_EXPERT), BF16), pltpu.VMEM((d, D_EXPERT), BF16),
                            pltpu.VMEM((D_EXPERT, d), BF16)]),
        compiler_params=_cparams(("arbitrary",)),
        name="expert_ffn",
    )(tile_expert, n_used, xg, wg, wu, wd)


def _combine_kernel(d0_ref, d1_ref, x_ref, rf_ref, g2_ref, fg_ref, y_ref, o_ref, buf, sem, *, final):
    tm = x_ref.shape[0]
    base = pl.program_id(0) * tm

    def issue(r, carry):
        _row_copy(y_ref.at[pl.ds(d0_ref[base + r], 1), :], buf.at[0, pl.ds(r, 1), :], sem.at[0]).start()
        _row_copy(y_ref.at[pl.ds(d1_ref[base + r], 1), :], buf.at[1, pl.ds(r, 1), :], sem.at[0]).start()
        return carry
    lax.fori_loop(0, tm, issue, 0, unroll=DMA_UNROLL)

    def drain(r, carry):
        _row_copy(y_ref.at[pl.ds(d0_ref[base + r], 1), :], buf.at[0, pl.ds(r, 1), :], sem.at[0]).wait()
        _row_copy(y_ref.at[pl.ds(d1_ref[base + r], 1), :], buf.at[1, pl.ds(r, 1), :], sem.at[0]).wait()
        return carry
    lax.fori_loop(0, tm, drain, 0, unroll=DMA_UNROLL)

    w0 = rf_ref[:, 0:1]
    w1 = rf_ref[:, 1:2]
    xn = x_ref[...] + g2_ref[0] * (w0 * buf[0] + w1 * buf[1])
    if final:
        xn = (xn * _rms(xn)) * fg_ref[...]
    o_ref[...] = xn


def _combine(d0, d1, x2, rf, g2, final_g, y, seq, final):
    t, d = x2.shape
    tm = MOVE_ROW_TILE
    per_b = seq // tm
    return pl.pallas_call(
        functools.partial(_combine_kernel, final=final),
        out_shape=jax.ShapeDtypeStruct((t, d), F32),
        grid_spec=pltpu.PrefetchScalarGridSpec(
            num_scalar_prefetch=2, grid=(t // tm,),
            in_specs=[pl.BlockSpec((tm, d), lambda i, a, b: (i, 0)),
                      pl.BlockSpec((tm, LANES), lambda i, a, b: (i, 0)),
                      pl.BlockSpec((1, 1, d), lambda i, a, b: (i // per_b, 0, 0)),
                      pl.BlockSpec((1, d), lambda i, a, b: (0, 0)),
                      pl.BlockSpec(memory_space=pl.ANY)],
            out_specs=pl.BlockSpec((tm, d), lambda i, a, b: (i, 0)),
            scratch_shapes=[pltpu.VMEM((2, tm, d), F32), pltpu.SemaphoreType.DMA((1,))]),
        compiler_params=_cparams(("arbitrary",)),
        name="combine",
    )(d0, d1, x2, rf, g2, final_g, y)


def _routing_plan(ri, cnt, n_tiles):
    tr = EXPERT_ROW_TILE
    counts = cnt[0, :N_EXPERTS].astype(I32)
    tiles_e = (counts + tr - 1) // tr
    tile_end = jnp.cumsum(tiles_e)
    row_off = (tile_end - tiles_e) * tr
    d0 = jnp.take(row_off, ri[:, 0]) + ri[:, 2]
    d1 = jnp.take(row_off, ri[:, 1]) + ri[:, 3]
    n_used = tile_end[-1]
    tid = jnp.minimum(jnp.arange(n_tiles, dtype=I32), n_used - 1)
    tile_expert = jnp.sum((tid[:, None] >= tile_end[None, :]).astype(I32), axis=1)
    return d0, d1, tile_expert.astype(I32), n_used.reshape(1).astype(I32)


def kernel(x, c, rel_bias, w_ada, b_ada, norm1_g, norm2_g, w_qkv, attn_sinks, swa_out_g, moba_out_g,
           w_o, w_group, b_group, w_expert_router, b_expert_router, w_gate, w_up, w_down, final_g):
    b, s, d = x.shape
    depth = w_ada.shape[0]
    t = b * s
    nblk = s // MOBA_BLOCK

    w = SWA_WINDOW
    swa_dist = (w + jnp.arange(w, dtype=I32))[:, None] - jnp.arange(2 * w, dtype=I32)[None, :]
    swa_bkt = _rel_bucket(swa_dist)[None]
    n_tab = min(nblk, FAR_TILE + 1)
    pos = jnp.arange(MOBA_BLOCK, dtype=I32)
    moba_dist = (jnp.arange(n_tab, dtype=I32)[:, None, None] * MOBA_BLOCK
                 + pos[None, None, :] - pos[None, :, None])
    moba_bkt = jnp.where(moba_dist >= 0, _rel_bucket(moba_dist), REL_BUCKETS)
    swa_tab = _bias_table(rel_bias, swa_bkt, SWA_Q_HEADS, 0)
    moba_tab = _bias_table(rel_bias, moba_bkt, MOBA_HEADS, SWA_Q_HEADS, scale=LOG2E)

    c_pad = jnp.zeros((8, d), F32).at[:b].set(c)
    mod = _ada_mod(c_pad, w_ada, b_ada)[:, :b]

    w_qkv_bf = w_qkv.astype(BF16)
    w_o_bf = w_o.astype(BF16)
    wg_all = w_gate.reshape(depth * N_EXPERTS, d, D_EXPERT)
    wu_all = w_up.reshape(depth * N_EXPERTS, d, D_EXPERT)
    wd_all = w_down.reshape(depth * N_EXPERTS, D_EXPERT, d)

    n_tiles = (t * 2) // EXPERT_ROW_TILE + N_EXPERTS
    xg = jnp.zeros((n_tiles * EXPERT_ROW_TILE, d), F32)
    x2 = x.reshape(t, d)
    fg = final_g.reshape(1, d)
    for l in range(depth):
        sh1, sc1, g1, sh2, sc2, g2 = [mod[l, :, k * d:(k + 1) * d].reshape(b, 1, d) for k in range(6)]
        qkv = _qkv_proj(x2, sh1, sc1, norm1_g[l].reshape(1, d), w_qkv_bf[l], s)
        qkv3 = qkv.reshape(b, s, D_QKV)
        ya = _swa_attention(qkv3, attn_sinks[l], swa_tab, swa_out_g[l].reshape(1, D_SWA))
        yb = _moba_attention(qkv3, moba_tab)
        wr = jnp.zeros((d, LANES), F32).at[:, :N_GROUPS].set(w_group[l])
        wr = wr.at[:, N_GROUPS:N_GROUPS + N_EXPERTS].set(w_expert_router[l])
        wr_hi = wr.astype(BF16)
        wr_lo = (wr - wr_hi.astype(F32)).astype(BF16)
        br = jnp.zeros((1, LANES), F32).at[0, :N_GROUPS].set(b_group[l])
        br = br.at[0, N_GROUPS:N_GROUPS + N_EXPERTS].set(b_expert_router[l])
        x2, h2, logits = _oproj(ya.reshape(t, D_SWA), yb.reshape(t, D_MOBA), x2,
                                moba_out_g[l].reshape(1, D_MOBA), w_o_bf[l], g1,
                                norm2_g[l].reshape(1, d), sc2, sh2, wr_hi, wr_lo, br, s)
        ri, rf, cnt = _route(logits)
        d0, d1, tile_expert, n_used = _routing_plan(ri, cnt, n_tiles)
        xg = _dispatch(d0, d1, h2, xg)
        y = _expert_ffn(tile_expert, n_used, xg, wg_all, wu_all, wd_all, l)
        x2 = _combine(d0, d1, x2, rf, g2, fg, y, s, final=(l == depth - 1))
    return x2.reshape(b, s, d)
```

```python
import functools
import math

import jax
import jax.numpy as jnp
from jax import lax
from jax.experimental import pallas as pl
from jax.experimental.pallas import tpu as pltpu

F32 = jnp.float32
BF16 = jnp.bfloat16
I32 = jnp.int32
HIGHEST = lax.Precision.HIGHEST

D_MODEL = 2048
HEAD_DIM = 64
SWA_Q_HEADS = 16
SWA_KV_HEADS = 2
SWA_GROUP = SWA_Q_HEADS // SWA_KV_HEADS
SWA_WINDOW = 128
MOBA_HEADS = 16
MOBA_BLOCK = 256
MOBA_TOPK = 3
D_SWA = SWA_Q_HEADS * HEAD_DIM
D_SWA_KV = SWA_KV_HEADS * HEAD_DIM
D_MOBA = MOBA_HEADS * HEAD_DIM
D_QKV = D_SWA + 2 * D_SWA_KV + 3 * D_MOBA
REL_BUCKETS = 32
REL_MAX_DISTANCE = 2048
N_GROUPS = 4
EXPERTS_PER_GROUP = 8
N_EXPERTS = N_GROUPS * EXPERTS_PER_GROUP
D_EXPERT = 512
EPS = 1e-6
NEG = -1e30
SCALE = 1.0 / math.sqrt(HEAD_DIM)
LOG2E = math.log2(math.e)

LANES = 128
BF16_ROWS = 16
V_ROWS = HEAD_DIM + BF16_ROWS
FAR_TILE = REL_MAX_DISTANCE // MOBA_BLOCK + 1
QKV_COL_TILE = D_QKV // 2
QKV_ROW_TILE = 512
OPROJ_ROW_TILE = 256
ROUTE_ROW_TILE = 512
MOVE_ROW_TILE = 256
EXPERT_ROW_TILE = 256
ADA_COL_TILE = 1024
DMA_UNROLL = 8
VMEM_LIMIT = 56 * 1024 * 1024


def _cparams(sem, vmem=VMEM_LIMIT):
    return pltpu.CompilerParams(dimension_semantics=sem, vmem_limit_bytes=vmem)


def _rel_bucket(dist):
    n = jnp.maximum(dist, 0)
    max_exact = REL_BUCKETS // 2
    nf = jnp.maximum(n, 1).astype(F32)
    large = max_exact + (jnp.log(nf / max_exact) / math.log(REL_MAX_DISTANCE / max_exact)
                         * (REL_BUCKETS - max_exact)).astype(I32)
    large = jnp.minimum(large, REL_BUCKETS - 1)
    return jnp.where(n < max_exact, n, large)


def _silu(v):
    return v * (1.0 / (1.0 + jnp.exp(-v)))


def _ada_kernel(c_ref, w_ref, b_ref, o_ref):
    ca = _silu(c_ref[...])
    o_ref[0] = jnp.dot(ca, w_ref[0], precision=HIGHEST, preferred_element_type=F32) + b_ref[0]


def _ada_mod(c_pad, w_ada, b_ada):
    depth, d, n = w_ada.shape
    rows = c_pad.shape[0]
    return pl.pallas_call(
        _ada_kernel,
        out_shape=jax.ShapeDtypeStruct((depth, rows, n), F32),
        grid=(depth, n // ADA_COL_TILE),
        in_specs=[pl.BlockSpec((rows, d), lambda l, j: (0, 0)),
                  pl.BlockSpec((1, d, ADA_COL_TILE), lambda l, j: (l, 0, j)),
                  pl.BlockSpec((1, 1, ADA_COL_TILE), lambda l, j: (l, 0, j))],
        out_specs=pl.BlockSpec((1, rows, ADA_COL_TILE), lambda l, j: (l, 0, j)),
        compiler_params=_cparams(("arbitrary", "arbitrary")),
        name="ada_mod",
    )(c_pad, w_ada, b_ada.reshape(depth, 1, n))


def _bias_kernel(rb_ref, bkt_ref, o_ref, *, head0, scale):
    h = pl.program_id(0) + head0
    b = bkt_ref[0]
    acc = jnp.full(b.shape, NEG, F32)
    for k in range(REL_BUCKETS):
        acc = jnp.where(b == k, rb_ref[k, h] * scale, acc)
    o_ref[0, 0] = acc


def _bias_table(rel_bias, bkt, n_heads, head0, scale=1.0, group=1):
    nt, r, c = bkt.shape
    return pl.pallas_call(
        functools.partial(_bias_kernel, head0=head0, scale=scale),
        out_shape=jax.ShapeDtypeStruct((n_heads // group, nt, r, c * group), F32),
        grid=(n_heads, nt),
        in_specs=[pl.BlockSpec(memory_space=pltpu.SMEM),
                  pl.BlockSpec((1, r, c), lambda h, t: (t, 0, 0))],
        out_specs=pl.BlockSpec((1, 1, r, c), lambda h, t: (h // group, t, 0, h % group)),
        compiler_params=_cparams(("arbitrary", "arbitrary")),
        name="bias_table",
    )(rel_bias, bkt)


def _rms(v):
    return lax.rsqrt(jnp.mean(v * v, axis=-1, keepdims=True) + EPS)


def _qkv_kernel(x_ref, sh_ref, sc_ref, g_ref, w_ref, o_ref):
    x = x_ref[...]
    h = (x * _rms(x)) * g_ref[...] * (1.0 + sc_ref[0]) + sh_ref[0]
    o_ref[...] = jnp.dot(h.astype(BF16), w_ref[...], preferred_element_type=F32).astype(BF16)


def _qkv_proj(x2, sh, sc, g, w_bf, seq):
    t, d = x2.shape
    tm = QKV_ROW_TILE
    per_b = seq // tm
    return pl.pallas_call(
        _qkv_kernel,
        out_shape=jax.ShapeDtypeStruct((t, D_QKV), BF16),
        grid=(D_QKV // QKV_COL_TILE, t // tm),
        in_specs=[pl.BlockSpec((tm, d), lambda j, i: (i, 0)),
                  pl.BlockSpec((1, 1, d), lambda j, i: (i // per_b, 0, 0)),
                  pl.BlockSpec((1, 1, d), lambda j, i: (i // per_b, 0, 0)),
                  pl.BlockSpec((1, d), lambda j, i: (0, 0)),
                  pl.BlockSpec((d, QKV_COL_TILE), lambda j, i: (0, j))],
        out_specs=pl.BlockSpec((tm, QKV_COL_TILE), lambda j, i: (i, j)),
        compiler_params=_cparams(("arbitrary", "arbitrary")),
        name="qkv_proj",
    )(x2, sh, sc, g, w_bf)


def _swa_kernel(sink_ref, q_ref, kvc_ref, kvp_ref, bias_ref, g_ref, o_ref):
    n = pl.program_id(1)
    w = SWA_WINDOW
    tsel = jnp.minimum(n, 1)
    qt = (q_ref[0].astype(F32) * (SCALE * LOG2E)).T.astype(BF16)
    kv = jnp.concatenate([kvp_ref[0], kvc_ref[0]], axis=0)
    k_all = kv[:, :D_SWA_KV]
    vt = kv[:, D_SWA_KV:].astype(F32).T.astype(BF16)
    zeros = jnp.zeros((HEAD_DIM, SWA_GROUP * w), BF16)
    outs = []
    for kh in range(SWA_KV_HEADS):
        h0 = kh * SWA_GROUP
        q_kh = jnp.concatenate([qt[(h0 + g) * HEAD_DIM:(h0 + g + 1) * HEAD_DIM, :]
                                for g in range(SWA_GROUP)], axis=1)
        rhs = jnp.concatenate([q_kh if j == kh else zeros for j in range(SWA_KV_HEADS)], axis=0)
        st = jnp.dot(k_all, rhs, preferred_element_type=F32) + bias_ref[kh, tsel]
        sink = jnp.concatenate([jnp.full((1, w), sink_ref[h0 + g] * LOG2E, F32)
                                for g in range(SWA_GROUP)], axis=1)
        m = jnp.maximum(jnp.max(st, axis=0, keepdims=True), sink)
        e = jnp.exp2(st - m)
        denom = jnp.sum(e, axis=0, keepdims=True) + jnp.exp2(sink - m)
        pv = jnp.dot(vt[kh * HEAD_DIM:(kh + 1) * HEAD_DIM, :], e.astype(BF16),
                     preferred_element_type=F32)
        o = pv * (1.0 / denom)
        outs.extend(o[:, g * w:(g + 1) * w] for g in range(SWA_GROUP))
    y = jnp.concatenate(outs, axis=0).T
    o_ref[0] = ((y * _rms(y)) * g_ref[...]).astype(BF16)


def _swa_attention(qkv3, sinks, bias_tab, out_g):
    b, s, _ = qkv3.shape
    w = SWA_WINDOW
    kv_blk = D_SWA // (2 * D_SWA_KV)
    return pl.pallas_call(
        _swa_kernel,
        out_shape=jax.ShapeDtypeStruct((b, s, D_SWA), BF16),
        grid=(b, s // w),
        in_specs=[pl.BlockSpec(memory_space=pltpu.SMEM),
                  pl.BlockSpec((1, w, D_SWA), lambda bi, n: (bi, n, 0)),
                  pl.BlockSpec((1, w, 2 * D_SWA_KV), lambda bi, n: (bi, n, kv_blk)),
                  pl.BlockSpec((1, w, 2 * D_SWA_KV), lambda bi, n: (bi, jnp.maximum(n - 1, 0), kv_blk)),
                  pl.BlockSpec((SWA_KV_HEADS, 2, 2 * w, SWA_GROUP * w), lambda bi, n: (0, 0, 0, 0)),
                  pl.BlockSpec((1, D_SWA), lambda bi, n: (0, 0))],
        out_specs=pl.BlockSpec((1, w, D_SWA), lambda bi, n: (bi, n, 0)),
        compiler_params=_cparams(("arbitrary", "arbitrary")),
        name="swa_attention",
    )(sinks, qkv3, qkv3, qkv3, bias_tab, out_g)


def _moba_kernel(q_ref, k_ref, v_ref, bias_ref, o_ref,
                 vt_sc, kmean_sc, rhs_sc, sel_sc, s_sc, p_sc, mb_sc, al_sc, al2_sc, m_sc, acc_sc, *, nblk):
    c = pl.program_id(2)
    blk = MOBA_BLOCK
    far = bias_ref.shape[1] - 1

    @pl.when(c == 0)
    def _prep():
        def body(j, carry):
            off = pl.multiple_of(j * blk, blk)
            vt = v_ref[0, pl.ds(off, blk), :].astype(F32).T.astype(BF16)
            ones = jnp.ones((BF16_ROWS, blk), BF16)
            vt_sc[j] = jnp.concatenate([vt[:HEAD_DIM], ones, vt[HEAD_DIM:], ones], axis=0)
            kb = k_ref[0, pl.ds(off, blk), :].astype(F32)
            kmean_sc[pl.ds(j, 1), :] = jnp.sum(kb, axis=0, keepdims=True) * (1.0 / blk)
            return carry
        lax.fori_loop(0, nblk, body, 0)

    qt = q_ref[0].astype(F32).T
    row = lax.broadcasted_iota(I32, (2 * HEAD_DIM, blk), 0)
    blk_i = lax.broadcasted_iota(I32, (nblk, blk), 0)
    for hh in range(2):
        in_head = jnp.where(row >= hh * HEAD_DIM, jnp.where(row < (hh + 1) * HEAD_DIM, 1, 0), 0) > 0
        qt_h = jnp.where(in_head, qt, 0.0)
        rhs_sc[hh] = (qt_h * (SCALE * LOG2E)).astype(BF16)
        gate = jnp.dot(kmean_sc[...], qt_h, precision=HIGHEST, preferred_element_type=F32)
        gate = jnp.where(blk_i < c, gate, NEG)
        sel = jnp.where(blk_i == c, 1.0, 0.0)
        for _ in range(min(MOBA_TOPK, nblk)):
            mx = jnp.max(gate, axis=0, keepdims=True)
            idx = jnp.min(jnp.where(gate == mx, blk_i, nblk), axis=0, keepdims=True)
            hit = blk_i == idx
            sel = jnp.where(hit, jnp.where(blk_i < c, 1.0, sel), sel)
            gate = jnp.where(hit, -jnp.inf, gate)
        sel_sc[hh] = sel
        m_sc[hh] = jnp.full((1, blk), NEG, F32)
        acc_sc[hh] = jnp.zeros((V_ROWS, blk), F32)

    def stage_a(u):
        m = [m_sc[0], m_sc[1]]
        for ti in range(2):
            jt = 2 * u + ti
            off = pl.multiple_of(jt * blk, blk)
            kj = k_ref[0, pl.ds(off, blk), :]
            d = jnp.clip(c - jt, 0, far)
            for hh in range(2):
                st = jnp.dot(kj, rhs_sc[hh], preferred_element_type=F32) + bias_ref[hh, d]
                s_sc[ti, hh] = st
                chosen = sel_sc[hh, pl.ds(jt, 1), :] > 0.5
                m_big = jnp.maximum(m[hh], jnp.max(st, axis=0, keepdims=True))
                m_new = jnp.where(chosen, m_big, m[hh])
                mb_sc[ti, hh] = m_big
                al_sc[ti, hh] = jnp.exp2(m[hh] - m_new)
                m[hh] = m_new
        m_sc[0] = m[0]
        m_sc[1] = m[1]

    def stage_b_exp(u):
        for ti in range(2):
            jt = 2 * u + ti
            for hh in range(2):
                p_sc[ti, hh] = jnp.exp2(s_sc[ti, hh] - mb_sc[ti, hh]).astype(BF16)

    def stage_b_pv(u):
        for ti in range(2):
            jt = 2 * u + ti
            for hh in range(2):
                chosen = sel_sc[hh, pl.ds(jt, 1), :] > 0.5
                pv = jnp.dot(vt_sc[jt, pl.ds(hh * V_ROWS, V_ROWS), :], p_sc[ti, hh],
                             preferred_element_type=F32)
                acc_sc[hh] = al2_sc[ti, hh] * acc_sc[hh] + jnp.where(chosen, pv, 0.0)

    n_pairs = (c + 2) // 2
    stage_a(0)

    def body(u, carry):
        stage_b_exp(u - 1)
        al2_sc[...] = al_sc[...]
        stage_a(u)
        stage_b_pv(u - 1)
        return carry
    lax.fori_loop(1, n_pairs, body, 0)
    stage_b_exp(n_pairs - 1)
    al2_sc[...] = al_sc[...]
    stage_b_pv(n_pairs - 1)

    outs = []
    for hh in range(2):
        inv_l = 1.0 / acc_sc[hh, HEAD_DIM:HEAD_DIM + 1, :]
        outs.append(acc_sc[hh, :HEAD_DIM, :] * inv_l)
    o_ref[0] = jnp.concatenate(outs, axis=0).T


def _moba_attention(qkv3, bias_tab):
    b, s, _ = qkv3.shape
    blk = MOBA_BLOCK
    nblk = s // blk
    assert nblk % 2 == 0
    pairs = MOBA_HEADS // 2
    pw = 2 * HEAD_DIM
    q0 = (D_SWA + 2 * D_SWA_KV) // pw
    k0 = q0 + D_MOBA // pw
    v0 = k0 + D_MOBA // pw
    nt = bias_tab.shape[1]
    return pl.pallas_call(
        functools.partial(_moba_kernel, nblk=nblk),
        out_shape=jax.ShapeDtypeStruct((b, s, D_MOBA), F32),
        grid=(b, pairs, nblk),
        in_specs=[pl.BlockSpec((1, blk, pw), lambda bi, hp, c: (bi, c, q0 + hp)),
                  pl.BlockSpec((1, s, pw), lambda bi, hp, c: (bi, 0, k0 + hp)),
                  pl.BlockSpec((1, s, pw), lambda bi, hp, c: (bi, 0, v0 + hp)),
                  pl.BlockSpec((2, nt, blk, blk), lambda bi, hp, c: (hp, 0, 0, 0))],
        out_specs=pl.BlockSpec((1, blk, pw), lambda bi, hp, c: (bi, c, hp)),
        scratch_shapes=[pltpu.VMEM((nblk, 2 * V_ROWS, blk), BF16),
                        pltpu.VMEM((nblk, pw), F32),
                        pltpu.VMEM((2, pw, blk), BF16),
                        pltpu.VMEM((2, nblk, blk), F32),
                        pltpu.VMEM((2, 2, blk, blk), F32),
                        pltpu.VMEM((2, 2, blk, blk), BF16),
                        pltpu.VMEM((2, 2, 1, blk), F32),
                        pltpu.VMEM((2, 2, 1, blk), F32),
                        pltpu.VMEM((2, 2, 1, blk), F32),
                        pltpu.VMEM((2, 1, blk), F32),
                        pltpu.VMEM((2, V_ROWS, blk), F32)],
        compiler_params=_cparams(("arbitrary", "arbitrary", "arbitrary")),
        name="moba_attention",
    )(qkv3, qkv3, qkv3, bias_tab)


def _oproj_kernel(ya_ref, yb_ref, x_ref, gb_ref, wo_ref, g1_ref, n2g_ref, sc2_ref, sh2_ref,
                  wrh_ref, wrl_ref, br_ref, xo_ref, h2_ref, lg_ref):
    yb = yb_ref[...]
    ybn = ((yb * _rms(yb)) * gb_ref[...]).astype(BF16)
    y = (jnp.dot(ya_ref[...], wo_ref[0:D_SWA, :], preferred_element_type=F32)
         + jnp.dot(ybn, wo_ref[D_SWA:D_SWA + D_MOBA, :], preferred_element_type=F32))
    xn = x_ref[...] + g1_ref[0] * y
    xo_ref[...] = xn
    h2 = (xn * _rms(xn)) * n2g_ref[...] * (1.0 + sc2_ref[0]) + sh2_ref[0]
    h2_ref[...] = h2
    hi = h2.astype(BF16)
    lo = (h2 - hi.astype(F32)).astype(BF16)
    lg_ref[...] = (jnp.dot(hi, wrh_ref[...], preferred_element_type=F32)
                   + jnp.dot(lo, wrh_ref[...], preferred_element_type=F32)
                   + jnp.dot(hi, wrl_ref[...], preferred_element_type=F32) + br_ref[...])


def _oproj(ya, yb, x2, gb, wo_bf, g1, n2g, sc2, sh2, wr_hi, wr_lo, br, seq):
    t, d = x2.shape
    tm = OPROJ_ROW_TILE
    per_b = seq // tm
    row = lambda i: (i, 0)
    fix = lambda i: (0, 0)
    bat = lambda i: (i // per_b, 0, 0)
    return pl.pallas_call(
        _oproj_kernel,
        out_shape=(jax.ShapeDtypeStruct((t, d), F32), jax.ShapeDtypeStruct((t, d), F32),
                   jax.ShapeDtypeStruct((t, LANES), F32)),
        grid=(t // tm,),
        in_specs=[pl.BlockSpec((tm, D_SWA), row), pl.BlockSpec((tm, D_MOBA), row),
                  pl.BlockSpec((tm, d), row), pl.BlockSpec((1, D_MOBA), fix),
                  pl.BlockSpec((D_SWA + D_MOBA, d), fix), pl.BlockSpec((1, 1, d), bat),
                  pl.BlockSpec((1, d), fix), pl.BlockSpec((1, 1, d), bat), pl.BlockSpec((1, 1, d), bat),
                  pl.BlockSpec((d, LANES), fix), pl.BlockSpec((d, LANES), fix), pl.BlockSpec((1, LANES), fix)],
        out_specs=(pl.BlockSpec((tm, d), row), pl.BlockSpec((tm, d), row), pl.BlockSpec((tm, LANES), row)),
        compiler_params=_cparams(("arbitrary",)),
        name="oproj",
    )(ya, yb, x2, gb, wo_bf, g1, n2g, sc2, sh2, wr_hi, wr_lo, br)


def _route_kernel(lg_ref, ri_ref, rf_ref, cnt_ref, carry_sc):
    i = pl.program_id(0)

    @pl.when(i == 0)
    def _init():
        carry_sc[...] = jnp.zeros_like(carry_sc)

    lg = lg_ref[...]
    tm = lg.shape[0]
    lane = lax.broadcasted_iota(I32, (tm, LANES), 1)
    ninf = -jnp.inf
    gl = jnp.where(lane < N_GROUPS, lg, ninf)
    ge = jnp.exp(gl - jnp.max(gl, axis=-1, keepdims=True))
    gp = ge / jnp.sum(ge, axis=-1, keepdims=True)
    p_grp = jnp.max(gp, axis=-1, keepdims=True)
    g_idx = jnp.min(jnp.where(gp == p_grp, lane, LANES), axis=-1, keepdims=True)
    lo = N_GROUPS + g_idx * EXPERTS_PER_GROUP
    in_grp = jnp.where(lane >= lo, jnp.where(lane < lo + EXPERTS_PER_GROUP, 1, 0), 0) > 0
    el = jnp.where(in_grp, lg, ninf)
    v0 = jnp.max(el, axis=-1, keepdims=True)
    i0 = jnp.min(jnp.where(el == v0, lane, LANES), axis=-1, keepdims=True)
    el = jnp.where(lane == i0, ninf, el)
    v1 = jnp.max(el, axis=-1, keepdims=True)
    i1 = jnp.min(jnp.where(el == v1, lane, LANES), axis=-1, keepdims=True)
    e0 = i0 - N_GROUPS
    e1 = i1 - N_GROUPS
    ex1 = jnp.exp(v1 - v0)
    den = 1.0 + ex1
    w0 = p_grp * (1.0 / den)
    w1 = p_grp * (ex1 / den)
    onehot = jnp.where(lane == e0, 1.0, jnp.where(lane == e1, 1.0, 0.0))
    r_i = lax.broadcasted_iota(I32, (tm, tm), 0)
    c_i = lax.broadcasted_iota(I32, (tm, tm), 1)
    tri = jnp.where(r_i > c_i, 1.0, 0.0).astype(BF16)
    cum = jnp.dot(tri, onehot.astype(BF16), preferred_element_type=F32) + carry_sc[...]
    pos0 = jnp.sum(jnp.where(lane == e0, cum, 0.0), axis=-1, keepdims=True).astype(I32)
    pos1 = jnp.sum(jnp.where(lane == e1, cum, 0.0), axis=-1, keepdims=True).astype(I32)
    carry_sc[...] = carry_sc[...] + jnp.sum(onehot, axis=0, keepdims=True)
    cnt_ref[...] = carry_sc[...]
    ri_ref[...] = jnp.where(lane == 0, e0, jnp.where(lane == 1, e1,
                            jnp.where(lane == 2, pos0, jnp.where(lane == 3, pos1, 0))))
    rf_ref[...] = jnp.where(lane == 0, w0, jnp.where(lane == 1, w1, 0.0))


def _route(logits):
    t = logits.shape[0]
    tm = ROUTE_ROW_TILE
    return pl.pallas_call(
        _route_kernel,
        out_shape=(jax.ShapeDtypeStruct((t, LANES), I32), jax.ShapeDtypeStruct((t, LANES), F32),
                   jax.ShapeDtypeStruct((1, LANES), F32)),
        grid=(t // tm,),
        in_specs=[pl.BlockSpec((tm, LANES), lambda i: (i, 0))],
        out_specs=(pl.BlockSpec((tm, LANES), lambda i: (i, 0)), pl.BlockSpec((tm, LANES), lambda i: (i, 0)),
                   pl.BlockSpec((1, LANES), lambda i: (0, 0))),
        scratch_shapes=[pltpu.VMEM((1, LANES), F32)],
        compiler_params=_cparams(("arbitrary",)),
        name="route",
    )(logits)


def _row_copy(src, dst, sem):
    return pltpu.make_async_copy(src, dst, sem)


def _dispatch_kernel(d0_ref, d1_ref, h_ref, xin_ref, xg_ref, sem):
    del xin_ref
    tm = h_ref.shape[0]
    base = pl.program_id(0) * tm

    def issue(r, carry):
        src = h_ref.at[pl.ds(r, 1), :]
        _row_copy(src, xg_ref.at[pl.ds(d0_ref[base + r], 1), :], sem.at[0]).start()
        _row_copy(src, xg_ref.at[pl.ds(d1_ref[base + r], 1), :], sem.at[0]).start()
        return carry
    lax.fori_loop(0, tm, issue, 0, unroll=DMA_UNROLL)

    def drain(r, carry):
        src = h_ref.at[pl.ds(r, 1), :]
        _row_copy(src, xg_ref.at[pl.ds(d0_ref[base + r], 1), :], sem.at[0]).wait()
        _row_copy(src, xg_ref.at[pl.ds(d1_ref[base + r], 1), :], sem.at[0]).wait()
        return carry
    lax.fori_loop(0, tm, drain, 0, unroll=DMA_UNROLL)


def _dispatch(d0, d1, h2, xg):
    t, d = h2.shape
    tm = MOVE_ROW_TILE
    return pl.pallas_call(
        _dispatch_kernel,
        out_shape=jax.ShapeDtypeStruct(xg.shape, xg.dtype),
        grid_spec=pltpu.PrefetchScalarGridSpec(
            num_scalar_prefetch=2, grid=(t // tm,),
            in_specs=[pl.BlockSpec((tm, d), lambda i, a, b: (i, 0)),
                      pl.BlockSpec(memory_space=pl.ANY)],
            out_specs=pl.BlockSpec(memory_space=pl.ANY),
            scratch_shapes=[pltpu.SemaphoreType.DMA((1,))]),
        input_output_aliases={3: 0},
        compiler_params=_cparams(("arbitrary",)),
        name="dispatch",
    )(d0, d1, h2, xg)


def _ffn_kernel(te_ref, nu_ref, x_ref, wg_ref, wu_ref, wd_ref, y_ref, wg_sc, wu_sc, wd_sc):
    i = pl.program_id(0)
    new_expert = jnp.logical_or(i == 0, te_ref[i] != te_ref[jnp.maximum(i - 1, 0)])

    @pl.when(new_expert)
    def _cast():
        wg_sc[...] = wg_ref[0].astype(BF16)
        wu_sc[...] = wu_ref[0].astype(BF16)
        wd_sc[...] = wd_ref[0].astype(BF16)

    @pl.when(i < nu_ref[0])
    def _compute():
        xb = x_ref[...].astype(BF16)
        g = jnp.dot(xb, wg_sc[...], preferred_element_type=F32)
        u = jnp.dot(xb, wu_sc[...], preferred_element_type=F32)
        hid = (_silu(g) * u).astype(BF16)
        y_ref[...] = jnp.dot(hid, wd_sc[...], preferred_element_type=F32)

    @pl.when(i >= nu_ref[0])
    def _skip():
        y_ref[...] = jnp.zeros_like(y_ref)


def _expert_ffn(tile_expert, n_used, xg, wg, wu, wd, layer):
    npad, d = xg.shape
    tr = EXPERT_ROW_TILE
    e_base = layer * N_EXPERTS
    wmap = lambda i, te, nu: (e_base + te[i], 0, 0)
    return pl.pallas_call(
        _ffn_kernel,
        out_shape=jax.ShapeDtypeStruct((npad, d), F32),
        grid_spec=pltpu.PrefetchScalarGridSpec(
            num_scalar_prefetch=2, grid=(npad // tr,),
            in_specs=[pl.BlockSpec((tr, d), lambda i, te, nu: (i, 0)),
                      pl.BlockSpec((1, d, D_EXPERT), wmap),
                      pl.BlockSpec((1, d, D_EXPERT), wmap),
                      pl.BlockSpec((1, D_EXPERT, d), wmap)],
            out_specs=pl.BlockSpec((tr, d), lambda i, te, nu: (i, 0)),
            scratch_shapes=[pltpu.VMEM((d, D_EXPERT), BF16), pltpu.VMEM((d, D_EXPERT), BF16),
                            pltpu.VMEM((D_EXPERT, d), BF16)]),
        compiler_params=_cparams(("arbitrary",)),
        name="expert_ffn",
    )(tile_expert, n_used, xg, wg, wu, wd)


def _combine_kernel(d0_ref, d1_ref, x_ref, rf_ref, g2_ref, fg_ref, y_ref, o_ref, buf, sem, *, final):
    tm = x_ref.shape[0]
    base = pl.program_id(0) * tm

    def issue(r, carry):
        _row_copy(y_ref.at[pl.ds(d0_ref[base + r], 1), :], buf.at[0, pl.ds(r, 1), :], sem.at[0]).start()
        _row_copy(y_ref.at[pl.ds(d1_ref[base + r], 1), :], buf.at[1, pl.ds(r, 1), :], sem.at[0]).start()
        return carry
    lax.fori_loop(0, tm, issue, 0, unroll=DMA_UNROLL)

    def drain(r, carry):
        _row_copy(y_ref.at[pl.ds(d0_ref[base + r], 1), :], buf.at[0, pl.ds(r, 1), :], sem.at[0]).wait()
        _row_copy(y_ref.at[pl.ds(d1_ref[base + r], 1), :], buf.at[1, pl.ds(r, 1), :], sem.at[0]).wait()
        return carry
    lax.fori_loop(0, tm, drain, 0, unroll=DMA_UNROLL)

    w0 = rf_ref[:, 0:1]
    w1 = rf_ref[:, 1:2]
    xn = x_ref[...] + g2_ref[0] * (w0 * buf[0] + w1 * buf[1])
    if final:
        xn = (xn * _rms(xn)) * fg_ref[...]
    o_ref[...] = xn


def _combine(d0, d1, x2, rf, g2, final_g, y, seq, final):
    t, d = x2.shape
    tm = MOVE_ROW_TILE
    per_b = seq // tm
    return pl.pallas_call(
        functools.partial(_combine_kernel, final=final),
        out_shape=jax.ShapeDtypeStruct((t, d), F32),
        grid_spec=pltpu.PrefetchScalarGridSpec(
            num_scalar_prefetch=2, grid=(t // tm,),
            in_specs=[pl.BlockSpec((tm, d), lambda i, a, b: (i, 0)),
                      pl.BlockSpec((tm, LANES), lambda i, a, b: (i, 0)),
                      pl.BlockSpec((1, 1, d), lambda i, a, b: (i // per_b, 0, 0)),
                      pl.BlockSpec((1, d), lambda i, a, b: (0, 0)),
                      pl.BlockSpec(memory_space=pl.ANY)],
            out_specs=pl.BlockSpec((tm, d), lambda i, a, b: (i, 0)),
            scratch_shapes=[pltpu.VMEM((2, tm, d), F32), pltpu.SemaphoreType.DMA((1,))]),
        compiler_params=_cparams(("arbitrary",)),
        name="combine",
    )(d0, d1, x2, rf, g2, final_g, y)


def _routing_plan(ri, cnt, n_tiles):
    tr = EXPERT_ROW_TILE
    counts = cnt[0, :N_EXPERTS].astype(I32)
    tiles_e = (counts + tr - 1) // tr
    tile_end = jnp.cumsum(tiles_e)
    row_off = (tile_end - tiles_e) * tr
    d0 = jnp.take(row_off, ri[:, 0]) + ri[:, 2]
    d1 = jnp.take(row_off, ri[:, 1]) + ri[:, 3]
    n_used = tile_end[-1]
    tid = jnp.minimum(jnp.arange(n_tiles, dtype=I32), n_used - 1)
    tile_expert = jnp.sum((tid[:, None] >= tile_end[None, :]).astype(I32), axis=1)
    return d0, d1, tile_expert.astype(I32), n_used.reshape(1).astype(I32)


def kernel(x, c, rel_bias, w_ada, b_ada, norm1_g, norm2_g, w_qkv, attn_sinks, swa_out_g, moba_out_g,
           w_o, w_group, b_group, w_expert_router, b_expert_router, w_gate, w_up, w_down, final_g):
    b, s, d = x.shape
    depth = w_ada.shape[0]
    t = b * s
    nblk = s // MOBA_BLOCK

    w = SWA_WINDOW
    key = jnp.arange(2 * w, dtype=I32)[:, None]
    swa_dist = (w + jnp.arange(w, dtype=I32))[None, :] - key
    band = (swa_dist >= 0) & (swa_dist < SWA_WINDOW)
    first = jnp.stack([band & (key >= w), band])
    swa_bkt = jnp.where(first, _rel_bucket(swa_dist)[None], REL_BUCKETS)
    n_tab = min(nblk, FAR_TILE + 1)
    pos = jnp.arange(MOBA_BLOCK, dtype=I32)
    moba_dist = (jnp.arange(n_tab, dtype=I32)[:, None, None] * MOBA_BLOCK
                 + pos[None, None, :] - pos[None, :, None])
    moba_bkt = jnp.where(moba_dist >= 0, _rel_bucket(moba_dist), REL_BUCKETS)
    swa_tab = _bias_table(rel_bias, swa_bkt, SWA_Q_HEADS, 0, scale=LOG2E, group=SWA_GROUP)
    moba_tab = _bias_table(rel_bias, moba_bkt, MOBA_HEADS, SWA_Q_HEADS, scale=LOG2E)

    c_pad = jnp.zeros((8, d), F32).at[:b].set(c)
    mod = _ada_mod(c_pad, w_ada, b_ada)[:, :b]

    w_qkv_bf = w_qkv.astype(BF16)
    w_o_bf = w_o.astype(BF16)
    wg_all = w_gate.reshape(depth * N_EXPERTS, d, D_EXPERT)
    wu_all = w_up.reshape(depth * N_EXPERTS, d, D_EXPERT)
    wd_all = w_down.reshape(depth * N_EXPERTS, D_EXPERT, d)

    n_tiles = (t * 2) // EXPERT_ROW_TILE + N_EXPERTS
    xg = jnp.zeros((n_tiles * EXPERT_ROW_TILE, d), F32)
    x2 = x.reshape(t, d)
    fg = final_g.reshape(1, d)
    for l in range(depth):
        sh1, sc1, g1, sh2, sc2, g2 = [mod[l, :, k * d:(k + 1) * d].reshape(b, 1, d) for k in range(6)]
        qkv = _qkv_proj(x2, sh1, sc1, norm1_g[l].reshape(1, d), w_qkv_bf[l], s)
        qkv3 = qkv.reshape(b, s, D_QKV)
        ya = _swa_attention(qkv3, attn_sinks[l], swa_tab, swa_out_g[l].reshape(1, D_SWA))
        yb = _moba_attention(qkv3, moba_tab)
        wr = jnp.zeros((d, LANES), F32).at[:, :N_GROUPS].set(w_group[l])
        wr = wr.at[:, N_GROUPS:N_GROUPS + N_EXPERTS].set(w_expert_router[l])
        wr_hi = wr.astype(BF16)
        wr_lo = (wr - wr_hi.astype(F32)).astype(BF16)
        br = jnp.zeros((1, LANES), F32).at[0, :N_GROUPS].set(b_group[l])
        br = br.at[0, N_GROUPS:N_GROUPS + N_EXPERTS].set(b_expert_router[l])
        x2, h2, logits = _oproj(ya.reshape(t, D_SWA), yb.reshape(t, D_MOBA), x2,
                                moba_out_g[l].reshape(1, D_MOBA), w_o_bf[l], g1,
                                norm2_g[l].reshape(1, d), sc2, sh2, wr_hi, wr_lo, br, s)
        ri, rf, cnt = _route(logits)
        d0, d1, tile_expert, n_used = _routing_plan(ri, cnt, n_tiles)
        xg = _dispatch(d0, d1, h2, xg)
        y = _expert_ffn(tile_expert, n_used, xg, wg_all, wu_all, wd_all, l)
        x2 = _combine(d0, d1, x2, rf, g2, fg, y, s, final=(l == depth - 1))
    return x2.reshape(b, s, d)
```

```python
import functools
import math

import jax
import jax.numpy as jnp
from jax import lax
from jax.experimental import pallas as pl
from jax.experimental.pallas import tpu as pltpu

F32 = jnp.float32
BF16 = jnp.bfloat16
I32 = jnp.int32
HIGHEST = lax.Precision.HIGHEST

D_MODEL = 2048
HEAD_DIM = 64
SWA_Q_HEADS = 16
SWA_KV_HEADS = 2
SWA_GROUP = SWA_Q_HEADS // SWA_KV_HEADS
SWA_WINDOW = 128
MOBA_HEADS = 16
MOBA_BLOCK = 256
MOBA_TOPK = 3
D_SWA = SWA_Q_HEADS * HEAD_DIM
D_SWA_KV = SWA_KV_HEADS * HEAD_DIM
D_MOBA = MOBA_HEADS * HEAD_DIM
D_QKV = D_SWA + 2 * D_SWA_KV + 3 * D_MOBA
REL_BUCKETS = 32
REL_MAX_DISTANCE = 2048
N_GROUPS = 4
EXPERTS_PER_GROUP = 8
N_EXPERTS = N_GROUPS * EXPERTS_PER_GROUP
D_EXPERT = 512
EPS = 1e-6
NEG = -1e30
SCALE = 1.0 / math.sqrt(HEAD_DIM)
LOG2E = math.log2(math.e)

LANES = 128
BF16_ROWS = 16
V_ROWS = HEAD_DIM + BF16_ROWS
MOBA_STEP_HEADS = 4
FAR_TILE = REL_MAX_DISTANCE // MOBA_BLOCK + 1
QKV_COL_TILE = D_QKV // 2
QKV_ROW_TILE = 512
OPROJ_ROW_TILE = 256
ROUTE_ROW_TILE = 512
MOVE_ROW_TILE = 256
EXPERT_ROW_TILE = 256
ADA_COL_TILE = 1024
DMA_UNROLL = 8
VMEM_LIMIT = 56 * 1024 * 1024


def _cparams(sem, vmem=VMEM_LIMIT):
    return pltpu.CompilerParams(dimension_semantics=sem, vmem_limit_bytes=vmem)


def _rel_bucket(dist):
    n = jnp.maximum(dist, 0)
    max_exact = REL_BUCKETS // 2
    nf = jnp.maximum(n, 1).astype(F32)
    large = max_exact + (jnp.log(nf / max_exact) / math.log(REL_MAX_DISTANCE / max_exact)
                         * (REL_BUCKETS - max_exact)).astype(I32)
    large = jnp.minimum(large, REL_BUCKETS - 1)
    return jnp.where(n < max_exact, n, large)


def _silu(v):
    return v * (1.0 / (1.0 + jnp.exp(-v)))


def _ada_kernel(c_ref, w_ref, b_ref, o_ref):
    ca = _silu(c_ref[...])
    o_ref[0] = jnp.dot(ca, w_ref[0], precision=HIGHEST, preferred_element_type=F32) + b_ref[0]


def _ada_mod(c_pad, w_ada, b_ada):
    depth, d, n = w_ada.shape
    rows = c_pad.shape[0]
    return pl.pallas_call(
        _ada_kernel,
        out_shape=jax.ShapeDtypeStruct((depth, rows, n), F32),
        grid=(depth, n // ADA_COL_TILE),
        in_specs=[pl.BlockSpec((rows, d), lambda l, j: (0, 0)),
                  pl.BlockSpec((1, d, ADA_COL_TILE), lambda l, j: (l, 0, j)),
                  pl.BlockSpec((1, 1, ADA_COL_TILE), lambda l, j: (l, 0, j))],
        out_specs=pl.BlockSpec((1, rows, ADA_COL_TILE), lambda l, j: (l, 0, j)),
        compiler_params=_cparams(("arbitrary", "arbitrary")),
        name="ada_mod",
    )(c_pad, w_ada, b_ada.reshape(depth, 1, n))


def _bias_kernel(rb_ref, bkt_ref, o_ref, *, head0, scale):
    h = pl.program_id(0) + head0
    b = bkt_ref[0]
    acc = jnp.full(b.shape, NEG, F32)
    for k in range(REL_BUCKETS):
        acc = jnp.where(b == k, rb_ref[k, h] * scale, acc)
    o_ref[0, 0] = acc


def _bias_table(rel_bias, bkt, n_heads, head0, scale=1.0, group=1):
    nt, r, c = bkt.shape
    return pl.pallas_call(
        functools.partial(_bias_kernel, head0=head0, scale=scale),
        out_shape=jax.ShapeDtypeStruct((n_heads // group, nt, r, c * group), F32),
        grid=(n_heads, nt),
        in_specs=[pl.BlockSpec(memory_space=pltpu.SMEM),
                  pl.BlockSpec((1, r, c), lambda h, t: (t, 0, 0))],
        out_specs=pl.BlockSpec((1, 1, r, c), lambda h, t: (h // group, t, 0, h % group)),
        compiler_params=_cparams(("arbitrary", "arbitrary")),
        name="bias_table",
    )(rel_bias, bkt)


def _rms(v):
    return lax.rsqrt(jnp.mean(v * v, axis=-1, keepdims=True) + EPS)


def _qkv_kernel(x_ref, sh_ref, sc_ref, g_ref, w_ref, o_ref):
    x = x_ref[...]
    h = (x * _rms(x)) * g_ref[...] * (1.0 + sc_ref[0]) + sh_ref[0]
    o_ref[...] = jnp.dot(h.astype(BF16), w_ref[...], preferred_element_type=F32).astype(BF16)


def _qkv_proj(x2, sh, sc, g, w_bf, seq):
    t, d = x2.shape
    tm = QKV_ROW_TILE
    per_b = seq // tm
    return pl.pallas_call(
        _qkv_kernel,
        out_shape=jax.ShapeDtypeStruct((t, D_QKV), BF16),
        grid=(D_QKV // QKV_COL_TILE, t // tm),
        in_specs=[pl.BlockSpec((tm, d), lambda j, i: (i, 0)),
                  pl.BlockSpec((1, 1, d), lambda j, i: (i // per_b, 0, 0)),
                  pl.BlockSpec((1, 1, d), lambda j, i: (i // per_b, 0, 0)),
                  pl.BlockSpec((1, d), lambda j, i: (0, 0)),
                  pl.BlockSpec((d, QKV_COL_TILE), lambda j, i: (0, j))],
        out_specs=pl.BlockSpec((tm, QKV_COL_TILE), lambda j, i: (i, j)),
        compiler_params=_cparams(("arbitrary", "arbitrary")),
        name="qkv_proj",
    )(x2, sh, sc, g, w_bf)


def _swa_kernel(sink_ref, q_ref, kvc_ref, kvp_ref, bias_ref, g_ref, o_ref):
    n = pl.program_id(1)
    w = SWA_WINDOW
    tsel = jnp.minimum(n, 1)
    qt = (q_ref[0].astype(F32) * (SCALE * LOG2E)).T.astype(BF16)
    kv = jnp.concatenate([kvp_ref[0], kvc_ref[0]], axis=0)
    k_all = kv[:, :D_SWA_KV]
    vt = kv[:, D_SWA_KV:].astype(F32).T.astype(BF16)
    zeros = jnp.zeros((HEAD_DIM, SWA_GROUP * w), BF16)
    outs = []
    for kh in range(SWA_KV_HEADS):
        h0 = kh * SWA_GROUP
        q_kh = jnp.concatenate([qt[(h0 + g) * HEAD_DIM:(h0 + g + 1) * HEAD_DIM, :]
                                for g in range(SWA_GROUP)], axis=1)
        rhs = jnp.concatenate([q_kh if j == kh else zeros for j in range(SWA_KV_HEADS)], axis=0)
        st = jnp.dot(k_all, rhs, preferred_element_type=F32) + bias_ref[kh, tsel]
        sink = jnp.concatenate([jnp.full((1, w), sink_ref[h0 + g] * LOG2E, F32)
                                for g in range(SWA_GROUP)], axis=1)
        m = jnp.maximum(jnp.max(st, axis=0, keepdims=True), sink)
        e = jnp.exp2(st - m)
        denom = jnp.sum(e, axis=0, keepdims=True) + jnp.exp2(sink - m)
        pv = jnp.dot(vt[kh * HEAD_DIM:(kh + 1) * HEAD_DIM, :], e.astype(BF16),
                     preferred_element_type=F32)
        o = pv * (1.0 / denom)
        outs.extend(o[:, g * w:(g + 1) * w] for g in range(SWA_GROUP))
    y = jnp.concatenate(outs, axis=0).T
    o_ref[0] = ((y * _rms(y)) * g_ref[...]).astype(BF16)


def _swa_attention(qkv3, sinks, bias_tab, out_g):
    b, s, _ = qkv3.shape
    w = SWA_WINDOW
    kv_blk = D_SWA // (2 * D_SWA_KV)
    return pl.pallas_call(
        _swa_kernel,
        out_shape=jax.ShapeDtypeStruct((b, s, D_SWA), BF16),
        grid=(b, s // w),
        in_specs=[pl.BlockSpec(memory_space=pltpu.SMEM),
                  pl.BlockSpec((1, w, D_SWA), lambda bi, n: (bi, n, 0)),
                  pl.BlockSpec((1, w, 2 * D_SWA_KV), lambda bi, n: (bi, n, kv_blk)),
                  pl.BlockSpec((1, w, 2 * D_SWA_KV), lambda bi, n: (bi, jnp.maximum(n - 1, 0), kv_blk)),
                  pl.BlockSpec((SWA_KV_HEADS, 2, 2 * w, SWA_GROUP * w), lambda bi, n: (0, 0, 0, 0)),
                  pl.BlockSpec((1, D_SWA), lambda bi, n: (0, 0))],
        out_specs=pl.BlockSpec((1, w, D_SWA), lambda bi, n: (bi, n, 0)),
        compiler_params=_cparams(("arbitrary", "arbitrary")),
        name="swa_attention",
    )(sinks, qkv3, qkv3, qkv3, bias_tab, out_g)


def _moba_kernel(q_ref, k_ref, v_ref, bias_ref, o_ref,
                 vt_sc, kmean_sc, rhs_sc, sel_sc, s_sc, p_sc, mb_sc, al_sc, al2_sc, m_sc, acc_sc, *, nblk):
    c = pl.program_id(2)
    blk = MOBA_BLOCK
    nh = MOBA_STEP_HEADS
    far = bias_ref.shape[1] - 1

    @pl.when(c == 0)
    def _prep():
        def body(j, carry):
            off = pl.multiple_of(j * blk, blk)
            vt = v_ref[0, pl.ds(off, blk), :].astype(F32).T.astype(BF16)
            ones = jnp.ones((BF16_ROWS, blk), BF16)
            parts = []
            for hh in range(nh):
                parts += [vt[hh * HEAD_DIM:(hh + 1) * HEAD_DIM], ones]
            vt_sc[j] = jnp.concatenate(parts, axis=0)
            kb = k_ref[0, pl.ds(off, blk), :].astype(F32)
            kmean_sc[pl.ds(j, 1), :] = jnp.sum(kb, axis=0, keepdims=True) * (1.0 / blk)
            return carry
        lax.fori_loop(0, nblk, body, 0)

    qt = q_ref[0].astype(F32).T
    row = lax.broadcasted_iota(I32, (nh * HEAD_DIM, blk), 0)
    blk_i = lax.broadcasted_iota(I32, (nblk, blk), 0)
    for hh in range(nh):
        in_head = jnp.where(row >= hh * HEAD_DIM, jnp.where(row < (hh + 1) * HEAD_DIM, 1, 0), 0) > 0
        qt_h = jnp.where(in_head, qt, 0.0)
        rhs_sc[hh] = (qt_h * (SCALE * LOG2E)).astype(BF16)
        gate = jnp.dot(kmean_sc[...], qt_h, precision=HIGHEST, preferred_element_type=F32)
        gate = jnp.where(blk_i < c, gate, NEG)
        sel = jnp.where(blk_i == c, 1.0, 0.0)
        for _ in range(min(MOBA_TOPK, nblk)):
            mx = jnp.max(gate, axis=0, keepdims=True)
            idx = jnp.min(jnp.where(gate == mx, blk_i, nblk), axis=0, keepdims=True)
            hit = blk_i == idx
            sel = jnp.where(hit, jnp.where(blk_i < c, 1.0, sel), sel)
            gate = jnp.where(hit, -jnp.inf, gate)
        sel_sc[hh] = sel
        m_sc[hh] = jnp.full((1, blk), NEG, F32)
        acc_sc[hh] = jnp.zeros((V_ROWS, blk), F32)

    def stage_a(u):
        m = [m_sc[hh] for hh in range(nh)]
        for ti in range(2):
            jt = 2 * u + ti
            off = pl.multiple_of(jt * blk, blk)
            kj = k_ref[0, pl.ds(off, blk), :]
            d = jnp.clip(c - jt, 0, far)
            for hh in range(nh):
                st = jnp.dot(kj, rhs_sc[hh], preferred_element_type=F32) + bias_ref[hh, d]
                s_sc[ti, hh] = st
                chosen = sel_sc[hh, pl.ds(jt, 1), :] > 0.5
                m_big = jnp.maximum(m[hh], jnp.max(st, axis=0, keepdims=True))
                m_new = jnp.where(chosen, m_big, m[hh])
                mb_sc[ti, hh] = m_big
                al_sc[ti, hh] = jnp.exp2(m[hh] - m_new)
                m[hh] = m_new
        for hh in range(nh):
            m_sc[hh] = m[hh]

    def stage_b_exp():
        for ti in range(2):
            for hh in range(nh):
                p_sc[ti, hh] = jnp.exp2(s_sc[ti, hh] - mb_sc[ti, hh]).astype(BF16)

    def stage_b_pv(u):
        for ti in range(2):
            jt = 2 * u + ti
            for hh in range(nh):
                chosen = sel_sc[hh, pl.ds(jt, 1), :] > 0.5
                pv = jnp.dot(vt_sc[jt, pl.ds(hh * V_ROWS, V_ROWS), :], p_sc[ti, hh],
                             preferred_element_type=F32)
                acc_sc[hh] = al2_sc[ti, hh] * acc_sc[hh] + jnp.where(chosen, pv, 0.0)

    n_pairs = (c + 2) // 2
    stage_a(0)

    def body(u, carry):
        stage_b_exp()
        al2_sc[...] = al_sc[...]
        stage_a(u)
        stage_b_pv(u - 1)
        return carry
    lax.fori_loop(1, n_pairs, body, 0)
    stage_b_exp()
    al2_sc[...] = al_sc[...]
    stage_b_pv(n_pairs - 1)

    outs = []
    for hh in range(nh):
        inv_l = 1.0 / acc_sc[hh, HEAD_DIM:HEAD_DIM + 1, :]
        outs.append(acc_sc[hh, :HEAD_DIM, :] * inv_l)
    o_ref[0] = jnp.concatenate(outs, axis=0).T


def _moba_attention(qkv3, bias_tab):
    b, s, _ = qkv3.shape
    blk = MOBA_BLOCK
    nblk = s // blk
    assert nblk % 2 == 0
    nh = MOBA_STEP_HEADS
    groups = MOBA_HEADS // nh
    pw = nh * HEAD_DIM
    q0 = (D_SWA + 2 * D_SWA_KV) // pw
    k0 = q0 + D_MOBA // pw
    v0 = k0 + D_MOBA // pw
    nt = bias_tab.shape[1]
    return pl.pallas_call(
        functools.partial(_moba_kernel, nblk=nblk),
        out_shape=jax.ShapeDtypeStruct((b, s, D_MOBA), F32),
        grid=(b, groups, nblk),
        in_specs=[pl.BlockSpec((1, blk, pw), lambda bi, hp, c: (bi, c, q0 + hp)),
                  pl.BlockSpec((1, s, pw), lambda bi, hp, c: (bi, 0, k0 + hp)),
                  pl.BlockSpec((1, s, pw), lambda bi, hp, c: (bi, 0, v0 + hp)),
                  pl.BlockSpec((nh, nt, blk, blk), lambda bi, hp, c: (hp, 0, 0, 0))],
        out_specs=pl.BlockSpec((1, blk, pw), lambda bi, hp, c: (bi, c, hp)),
        scratch_shapes=[pltpu.VMEM((nblk, nh * V_ROWS, blk), BF16),
                        pltpu.VMEM((nblk, pw), F32),
                        pltpu.VMEM((nh, pw, blk), BF16),
                        pltpu.VMEM((nh, nblk, blk), F32),
                        pltpu.VMEM((2, nh, blk, blk), F32),
                        pltpu.VMEM((2, nh, blk, blk), BF16),
                        pltpu.VMEM((2, nh, 1, blk), F32),
                        pltpu.VMEM((2, nh, 1, blk), F32),
                        pltpu.VMEM((2, nh, 1, blk), F32),
                        pltpu.VMEM((nh, 1, blk), F32),
                        pltpu.VMEM((nh, V_ROWS, blk), F32)],
        compiler_params=_cparams(("arbitrary", "arbitrary", "arbitrary")),
        name="moba_attention",
    )(qkv3, qkv3, qkv3, bias_tab)


def _oproj_kernel(ya_ref, yb_ref, x_ref, gb_ref, wo_ref, g1_ref, n2g_ref, sc2_ref, sh2_ref,
                  wr2_ref, br_ref, xo_ref, h2_ref, lg_ref):
    yb = yb_ref[...]
    ybn = ((yb * _rms(yb)) * gb_ref[...]).astype(BF16)
    y = (jnp.dot(ya_ref[...], wo_ref[0:D_SWA, :], preferred_element_type=F32)
         + jnp.dot(ybn, wo_ref[D_SWA:D_SWA + D_MOBA, :], preferred_element_type=F32))
    xn = x_ref[...] + g1_ref[0] * y
    xo_ref[...] = xn
    h2 = (xn * _rms(xn)) * n2g_ref[...] * (1.0 + sc2_ref[0]) + sh2_ref[0]
    h2_ref[...] = h2
    hi = h2.astype(BF16)
    lo = (h2 - hi.astype(F32)).astype(BF16)
    both = jnp.dot(hi, wr2_ref[...], preferred_element_type=F32)
    lg_ref[...] = (both[:, :LANES] + both[:, LANES:]
                   + jnp.dot(lo, wr2_ref[:, :LANES], preferred_element_type=F32) + br_ref[...])


def _oproj(ya, yb, x2, gb, wo_bf, g1, n2g, sc2, sh2, wr2, br, seq):
    t, d = x2.shape
    tm = OPROJ_ROW_TILE
    per_b = seq // tm
    row = lambda i: (i, 0)
    fix = lambda i: (0, 0)
    bat = lambda i: (i // per_b, 0, 0)
    return pl.pallas_call(
        _oproj_kernel,
        out_shape=(jax.ShapeDtypeStruct((t, d), F32), jax.ShapeDtypeStruct((t, d), F32),
                   jax.ShapeDtypeStruct((t, LANES), F32)),
        grid=(t // tm,),
        in_specs=[pl.BlockSpec((tm, D_SWA), row), pl.BlockSpec((tm, D_MOBA), row),
                  pl.BlockSpec((tm, d), row), pl.BlockSpec((1, D_MOBA), fix),
                  pl.BlockSpec((D_SWA + D_MOBA, d), fix), pl.BlockSpec((1, 1, d), bat),
                  pl.BlockSpec((1, d), fix), pl.BlockSpec((1, 1, d), bat), pl.BlockSpec((1, 1, d), bat),
                  pl.BlockSpec((d, 2 * LANES), fix), pl.BlockSpec((1, LANES), fix)],
        out_specs=(pl.BlockSpec((tm, d), row), pl.BlockSpec((tm, d), row), pl.BlockSpec((tm, LANES), row)),
        compiler_params=_cparams(("arbitrary",)),
        name="oproj",
    )(ya, yb, x2, gb, wo_bf, g1, n2g, sc2, sh2, wr2, br)


def _route_kernel(lg_ref, ri_ref, rf_ref, cnt_ref, carry_sc):
    i = pl.program_id(0)

    @pl.when(i == 0)
    def _init():
        carry_sc[...] = jnp.zeros_like(carry_sc)

    lg = lg_ref[...]
    tm = lg.shape[0]
    lane = lax.broadcasted_iota(I32, (tm, LANES), 1)
    ninf = -jnp.inf
    gl = jnp.where(lane < N_GROUPS, lg, ninf)
    ge = jnp.exp(gl - jnp.max(gl, axis=-1, keepdims=True))
    gp = ge / jnp.sum(ge, axis=-1, keepdims=True)
    p_grp = jnp.max(gp, axis=-1, keepdims=True)
    g_idx = jnp.min(jnp.where(gp == p_grp, lane, LANES), axis=-1, keepdims=True)
    lo = N_GROUPS + g_idx * EXPERTS_PER_GROUP
    in_grp = jnp.where(lane >= lo, jnp.where(lane < lo + EXPERTS_PER_GROUP, 1, 0), 0) > 0
    el = jnp.where(in_grp, lg, ninf)
    v0 = jnp.max(el, axis=-1, keepdims=True)
    i0 = jnp.min(jnp.where(el == v0, lane, LANES), axis=-1, keepdims=True)
    el = jnp.where(lane == i0, ninf, el)
    v1 = jnp.max(el, axis=-1, keepdims=True)
    i1 = jnp.min(jnp.where(el == v1, lane, LANES), axis=-1, keepdims=True)
    e0 = i0 - N_GROUPS
    e1 = i1 - N_GROUPS
    ex1 = jnp.exp(v1 - v0)
    den = 1.0 + ex1
    w0 = p_grp * (1.0 / den)
    w1 = p_grp * (ex1 / den)
    onehot = jnp.where(lane == e0, 1.0, jnp.where(lane == e1, 1.0, 0.0))
    r_i = lax.broadcasted_iota(I32, (tm, tm), 0)
    c_i = lax.broadcasted_iota(I32, (tm, tm), 1)
    tri = jnp.where(r_i > c_i, 1.0, 0.0).astype(BF16)
    cum = jnp.dot(tri, onehot.astype(BF16), preferred_element_type=F32) + carry_sc[...]
    pos0 = jnp.sum(jnp.where(lane == e0, cum, 0.0), axis=-1, keepdims=True).astype(I32)
    pos1 = jnp.sum(jnp.where(lane == e1, cum, 0.0), axis=-1, keepdims=True).astype(I32)
    carry_sc[...] = carry_sc[...] + jnp.sum(onehot, axis=0, keepdims=True)
    cnt_ref[...] = carry_sc[...]
    ri_ref[...] = jnp.where(lane == 0, e0, jnp.where(lane == 1, e1,
                            jnp.where(lane == 2, pos0, jnp.where(lane == 3, pos1, 0))))
    rf_ref[...] = jnp.where(lane == 0, w0, jnp.where(lane == 1, w1, 0.0))


def _route(logits):
    t = logits.shape[0]
    tm = ROUTE_ROW_TILE
    return pl.pallas_call(
        _route_kernel,
        out_shape=(jax.ShapeDtypeStruct((t, LANES), I32), jax.ShapeDtypeStruct((t, LANES), F32),
                   jax.ShapeDtypeStruct((1, LANES), F32)),
        grid=(t // tm,),
        in_specs=[pl.BlockSpec((tm, LANES), lambda i: (i, 0))],
        out_specs=(pl.BlockSpec((tm, LANES), lambda i: (i, 0)), pl.BlockSpec((tm, LANES), lambda i: (i, 0)),
                   pl.BlockSpec((1, LANES), lambda i: (0, 0))),
        scratch_shapes=[pltpu.VMEM((1, LANES), F32)],
        compiler_params=_cparams(("arbitrary",)),
        name="route",
    )(logits)


def _row_copy(src, dst, sem):
    return pltpu.make_async_copy(src, dst, sem)


def _dispatch_kernel(d0_ref, d1_ref, h_ref, xin_ref, xg_ref, sem):
    del xin_ref
    tm = h_ref.shape[0]
    base = pl.program_id(0) * tm

    def issue(r, carry):
        src = h_ref.at[pl.ds(r, 1), :]
        _row_copy(src, xg_ref.at[pl.ds(d0_ref[base + r], 1), :], sem.at[0]).start()
        _row_copy(src, xg_ref.at[pl.ds(d1_ref[base + r], 1), :], sem.at[0]).start()
        return carry
    lax.fori_loop(0, tm, issue, 0, unroll=DMA_UNROLL)

    def drain(r, carry):
        src = h_ref.at[pl.ds(r, 1), :]
        _row_copy(src, xg_ref.at[pl.ds(d0_ref[base + r], 1), :], sem.at[0]).wait()
        _row_copy(src, xg_ref.at[pl.ds(d1_ref[base + r], 1), :], sem.at[0]).wait()
        return carry
    lax.fori_loop(0, tm, drain, 0, unroll=DMA_UNROLL)


def _dispatch(d0, d1, h2, xg):
    t, d = h2.shape
    tm = MOVE_ROW_TILE
    return pl.pallas_call(
        _dispatch_kernel,
        out_shape=jax.ShapeDtypeStruct(xg.shape, xg.dtype),
        grid_spec=pltpu.PrefetchScalarGridSpec(
            num_scalar_prefetch=2, grid=(t // tm,),
            in_specs=[pl.BlockSpec((tm, d), lambda i, a, b: (i, 0)),
                      pl.BlockSpec(memory_space=pl.ANY)],
            out_specs=pl.BlockSpec(memory_space=pl.ANY),
            scratch_shapes=[pltpu.SemaphoreType.DMA((1,))]),
        input_output_aliases={3: 0},
        compiler_params=_cparams(("arbitrary",)),
        name="dispatch",
    )(d0, d1, h2, xg)


def _ffn_kernel(te_ref, nu_ref, x_ref, wg_ref, wu_ref, wd_ref, y_ref, wg_sc, wu_sc, wd_sc):
    i = pl.program_id(0)
    new_expert = jnp.logical_or(i == 0, te_ref[i] != te_ref[jnp.maximum(i - 1, 0)])

    @pl.when(new_expert)
    def _cast():
        wg_sc[...] = wg_ref[0].astype(BF16)
        wu_sc[...] = wu_ref[0].astype(BF16)
        wd_sc[...] = wd_ref[0].astype(BF16)

    @pl.when(i < nu_ref[0])
    def _compute():
        xb = x_ref[...].astype(BF16)
        g = jnp.dot(xb, wg_sc[...], preferred_element_type=F32)
        u = jnp.dot(xb, wu_sc[...], preferred_element_type=F32)
        hid = (_silu(g) * u).astype(BF16)
        y_ref[...] = jnp.dot(hid, wd_sc[...], preferred_element_type=F32)

    @pl.when(i >= nu_ref[0])
    def _skip():
        y_ref[...] = jnp.zeros_like(y_ref)


def _expert_ffn(tile_expert, n_used, xg, wg, wu, wd, layer):
    npad, d = xg.shape
    tr = EXPERT_ROW_TILE
    e_base = layer * N_EXPERTS
    wmap = lambda i, te, nu: (e_base + te[i], 0, 0)
    return pl.pallas_call(
        _ffn_kernel,
        out_shape=jax.ShapeDtypeStruct((npad, d), F32),
        grid_spec=pltpu.PrefetchScalarGridSpec(
            num_scalar_prefetch=2, grid=(npad // tr,),
            in_specs=[pl.BlockSpec((tr, d), lambda i, te, nu: (i, 0)),
                      pl.BlockSpec((1, d, D_EXPERT), wmap),
                      pl.BlockSpec((1, d, D_EXPERT), wmap),
                      pl.BlockSpec((1, D_EXPERT, d), wmap)],
            out_specs=pl.BlockSpec((tr, d), lambda i, te, nu: (i, 0)),
            scratch_shapes=[pltpu.VMEM((d, D_EXPERT), BF16), pltpu.VMEM((d, D_EXPERT), BF16),
                            pltpu.VMEM((D_EXPERT, d), BF16)]),
        compiler_params=_cparams(("arbitrary",)),
        name="expert_ffn",
    )(tile_expert, n_used, xg, wg, wu, wd)


def _combine_kernel(d0_ref, d1_ref, x_ref, rf_ref, g2_ref, fg_ref, y_ref, o_ref, buf, sem, *, final):
    tm = x_ref.shape[0]
    base = pl.program_id(0) * tm

    def issue(r, carry):
        _row_copy(y_ref.at[pl.ds(d0_ref[base + r], 1), :], buf.at[0, pl.ds(r, 1), :], sem.at[0]).start()
        _row_copy(y_ref.at[pl.ds(d1_ref[base + r], 1), :], buf.at[1, pl.ds(r, 1), :], sem.at[0]).start()
        return carry
    lax.fori_loop(0, tm, issue, 0, unroll=DMA_UNROLL)

    def drain(r, carry):
        _row_copy(y_ref.at[pl.ds(d0_ref[base + r], 1), :], buf.at[0, pl.ds(r, 1), :], sem.at[0]).wait()
        _row_copy(y_ref.at[pl.ds(d1_ref[base + r], 1), :], buf.at[1, pl.ds(r, 1), :], sem.at[0]).wait()
        return carry
    lax.fori_loop(0, tm, drain, 0, unroll=DMA_UNROLL)

    w0 = rf_ref[:, 0:1]
    w1 = rf_ref[:, 1:2]
    xn = x_ref[...] + g2_ref[0] * (w0 * buf[0] + w1 * buf[1])
    if final:
        xn = (xn * _rms(xn)) * fg_ref[...]
    o_ref[...] = xn


def _combine(d0, d1, x2, rf, g2, final_g, y, seq, final):
    t, d = x2.shape
    tm = MOVE_ROW_TILE
    per_b = seq // tm
    return pl.pallas_call(
        functools.partial(_combine_kernel, final=final),
        out_shape=jax.ShapeDtypeStruct((t, d), F32),
        grid_spec=pltpu.PrefetchScalarGridSpec(
            num_scalar_prefetch=2, grid=(t // tm,),
            in_specs=[pl.BlockSpec((tm, d), lambda i, a, b: (i, 0)),
                      pl.BlockSpec((tm, LANES), lambda i, a, b: (i, 0)),
                      pl.BlockSpec((1, 1, d), lambda i, a, b: (i // per_b, 0, 0)),
                      pl.BlockSpec((1, d), lambda i, a, b: (0, 0)),
                      pl.BlockSpec(memory_space=pl.ANY)],
            out_specs=pl.BlockSpec((tm, d), lambda i, a, b: (i, 0)),
            scratch_shapes=[pltpu.VMEM((2, tm, d), F32), pltpu.SemaphoreType.DMA((1,))]),
        compiler_params=_cparams(("arbitrary",)),
        name="combine",
    )(d0, d1, x2, rf, g2, final_g, y)


def _routing_plan(ri, cnt, n_tiles):
    tr = EXPERT_ROW_TILE
    counts = cnt[0, :N_EXPERTS].astype(I32)
    tiles_e = (counts + tr - 1) // tr
    tile_end = jnp.cumsum(tiles_e)
    row_off = (tile_end - tiles_e) * tr
    d0 = jnp.take(row_off, ri[:, 0]) + ri[:, 2]
    d1 = jnp.take(row_off, ri[:, 1]) + ri[:, 3]
    n_used = tile_end[-1]
    tid = jnp.minimum(jnp.arange(n_tiles, dtype=I32), n_used - 1)
    tile_expert = jnp.sum((tid[:, None] >= tile_end[None, :]).astype(I32), axis=1)
    return d0, d1, tile_expert.astype(I32), n_used.reshape(1).astype(I32)


def kernel(x, c, rel_bias, w_ada, b_ada, norm1_g, norm2_g, w_qkv, attn_sinks, swa_out_g, moba_out_g,
           w_o, w_group, b_group, w_expert_router, b_expert_router, w_gate, w_up, w_down, final_g):
    b, s, d = x.shape
    depth = w_ada.shape[0]
    t = b * s
    nblk = s // MOBA_BLOCK

    w = SWA_WINDOW
    key = jnp.arange(2 * w, dtype=I32)[:, None]
    swa_dist = (w + jnp.arange(w, dtype=I32))[None, :] - key
    band = (swa_dist >= 0) & (swa_dist < SWA_WINDOW)
    first = jnp.stack([band & (key >= w), band])
    swa_bkt = jnp.where(first, _rel_bucket(swa_dist)[None], REL_BUCKETS)
    n_tab = min(nblk, FAR_TILE + 1)
    pos = jnp.arange(MOBA_BLOCK, dtype=I32)
    moba_dist = (jnp.arange(n_tab, dtype=I32)[:, None, None] * MOBA_BLOCK
                 + pos[None, None, :] - pos[None, :, None])
    moba_bkt = jnp.where(moba_dist >= 0, _rel_bucket(moba_dist), REL_BUCKETS)
    swa_tab = _bias_table(rel_bias, swa_bkt, SWA_Q_HEADS, 0, scale=LOG2E, group=SWA_GROUP)
    moba_tab = _bias_table(rel_bias, moba_bkt, MOBA_HEADS, SWA_Q_HEADS, scale=LOG2E)

    c_pad = jnp.zeros((8, d), F32).at[:b].set(c)
    mod = _ada_mod(c_pad, w_ada, b_ada)[:, :b]

    w_qkv_bf = w_qkv.astype(BF16)
    w_o_bf = w_o.astype(BF16)
    wg_all = w_gate.reshape(depth * N_EXPERTS, d, D_EXPERT)
    wu_all = w_up.reshape(depth * N_EXPERTS, d, D_EXPERT)
    wd_all = w_down.reshape(depth * N_EXPERTS, D_EXPERT, d)

    n_tiles = (t * 2) // EXPERT_ROW_TILE + N_EXPERTS
    xg = jnp.zeros((n_tiles * EXPERT_ROW_TILE, d), F32)
    x2 = x.reshape(t, d)
    fg = final_g.reshape(1, d)
    for l in range(depth):
        sh1, sc1, g1, sh2, sc2, g2 = [mod[l, :, k * d:(k + 1) * d].reshape(b, 1, d) for k in range(6)]
        qkv = _qkv_proj(x2, sh1, sc1, norm1_g[l].reshape(1, d), w_qkv_bf[l], s)
        qkv3 = qkv.reshape(b, s, D_QKV)
        ya = _swa_attention(qkv3, attn_sinks[l], swa_tab, swa_out_g[l].reshape(1, D_SWA))
        yb = _moba_attention(qkv3, moba_tab)
        wr = jnp.zeros((d, LANES), F32).at[:, :N_GROUPS].set(w_group[l])
        wr = wr.at[:, N_GROUPS:N_GROUPS + N_EXPERTS].set(w_expert_router[l])
        wr_hi = wr.astype(BF16)
        wr_lo = (wr - wr_hi.astype(F32)).astype(BF16)
        wr2 = jnp.concatenate([wr_hi, wr_lo], axis=1)
        br = jnp.zeros((1, LANES), F32).at[0, :N_GROUPS].set(b_group[l])
        br = br.at[0, N_GROUPS:N_GROUPS + N_EXPERTS].set(b_expert_router[l])
        x2, h2, logits = _oproj(ya.reshape(t, D_SWA), yb.reshape(t, D_MOBA), x2,
                                moba_out_g[l].reshape(1, D_MOBA), w_o_bf[l], g1,
                                norm2_g[l].reshape(1, d), sc2, sh2, wr2, br, s)
        ri, rf, cnt = _route(logits)
        d0, d1, tile_expert, n_used = _routing_plan(ri, cnt, n_tiles)
        xg = _dispatch(d0, d1, h2, xg)
        y = _expert_ffn(tile_expert, n_used, xg, wg_all, wu_all, wd_all, l)
        x2 = _combine(d0, d1, x2, rf, g2, fg, y, s, final=(l == depth - 1))
    return x2.reshape(b, s, d)
```

```python
import functools
import math

import jax
import jax.numpy as jnp
from jax import lax
from jax.experimental import pallas as pl
from jax.experimental.pallas import tpu as pltpu

F32 = jnp.float32
BF16 = jnp.bfloat16
I32 = jnp.int32
HIGHEST = lax.Precision.HIGHEST

D_MODEL = 2048
HEAD_DIM = 64
SWA_Q_HEADS = 16
SWA_KV_HEADS = 2
SWA_GROUP = SWA_Q_HEADS // SWA_KV_HEADS
SWA_WINDOW = 128
MOBA_HEADS = 16
MOBA_BLOCK = 256
MOBA_TOPK = 3
D_SWA = SWA_Q_HEADS * HEAD_DIM
D_SWA_KV = SWA_KV_HEADS * HEAD_DIM
D_MOBA = MOBA_HEADS * HEAD_DIM
D_QKV = D_SWA + 2 * D_SWA_KV + 3 * D_MOBA
REL_BUCKETS = 32
REL_MAX_DISTANCE = 2048
N_GROUPS = 4
EXPERTS_PER_GROUP = 8
N_EXPERTS = N_GROUPS * EXPERTS_PER_GROUP
D_EXPERT = 512
EPS = 1e-6
NEG = -1e30
SCALE = 1.0 / math.sqrt(HEAD_DIM)
LOG2E = math.log2(math.e)

LANES = 128
BF16_ROWS = 16
V_ROWS = HEAD_DIM + BF16_ROWS
MOBA_STEP_HEADS = 4
FAR_TILE = REL_MAX_DISTANCE // MOBA_BLOCK + 1
QKV_COL_TILE = D_QKV // 2
QKV_ROW_TILE = 512
OPROJ_ROW_TILE = 256
ROUTE_ROW_TILE = 512
MOVE_ROW_TILE = 256
EXPERT_ROW_TILE = 256
ADA_COL_TILE = 1024
DMA_UNROLL = 8
VMEM_LIMIT = 56 * 1024 * 1024


def _cparams(sem, vmem=VMEM_LIMIT):
    return pltpu.CompilerParams(dimension_semantics=sem, vmem_limit_bytes=vmem)


def _rel_bucket(dist):
    n = jnp.maximum(dist, 0)
    max_exact = REL_BUCKETS // 2
    nf = jnp.maximum(n, 1).astype(F32)
    large = max_exact + (jnp.log(nf / max_exact) / math.log(REL_MAX_DISTANCE / max_exact)
                         * (REL_BUCKETS - max_exact)).astype(I32)
    large = jnp.minimum(large, REL_BUCKETS - 1)
    return jnp.where(n < max_exact, n, large)


def _silu(v):
    return v * (1.0 / (1.0 + jnp.exp(-v)))


def _ada_kernel(c_ref, w_ref, b_ref, o_ref):
    ca = _silu(c_ref[...])
    o_ref[0] = jnp.dot(ca, w_ref[0], precision=HIGHEST, preferred_element_type=F32) + b_ref[0]


def _ada_mod(c_pad, w_ada, b_ada):
    depth, d, n = w_ada.shape
    rows = c_pad.shape[0]
    return pl.pallas_call(
        _ada_kernel,
        out_shape=jax.ShapeDtypeStruct((depth, rows, n), F32),
        grid=(depth, n // ADA_COL_TILE),
        in_specs=[pl.BlockSpec((rows, d), lambda l, j: (0, 0)),
                  pl.BlockSpec((1, d, ADA_COL_TILE), lambda l, j: (l, 0, j)),
                  pl.BlockSpec((1, 1, ADA_COL_TILE), lambda l, j: (l, 0, j))],
        out_specs=pl.BlockSpec((1, rows, ADA_COL_TILE), lambda l, j: (l, 0, j)),
        compiler_params=_cparams(("arbitrary", "arbitrary")),
        name="ada_mod",
    )(c_pad, w_ada, b_ada.reshape(depth, 1, n))


def _bias_kernel(rb_ref, bkt_ref, o_ref, *, head0, scale):
    h = pl.program_id(0) + head0
    b = bkt_ref[0]
    acc = jnp.full(b.shape, NEG, F32)
    for k in range(REL_BUCKETS):
        acc = jnp.where(b == k, rb_ref[k, h] * scale, acc)
    o_ref[0, 0] = acc


def _bias_table(rel_bias, bkt, n_heads, head0, scale=1.0, group=1):
    nt, r, c = bkt.shape
    return pl.pallas_call(
        functools.partial(_bias_kernel, head0=head0, scale=scale),
        out_shape=jax.ShapeDtypeStruct((n_heads // group, nt, r, c * group), F32),
        grid=(n_heads, nt),
        in_specs=[pl.BlockSpec(memory_space=pltpu.SMEM),
                  pl.BlockSpec((1, r, c), lambda h, t: (t, 0, 0))],
        out_specs=pl.BlockSpec((1, 1, r, c), lambda h, t: (h // group, t, 0, h % group)),
        compiler_params=_cparams(("arbitrary", "arbitrary")),
        name="bias_table",
    )(rel_bias, bkt)


def _rms(v):
    return lax.rsqrt(jnp.mean(v * v, axis=-1, keepdims=True) + EPS)


def _qkv_kernel(x_ref, sh_ref, sc_ref, g_ref, w_ref, o_ref):
    x = x_ref[...]
    h = (x * _rms(x)) * g_ref[...] * (1.0 + sc_ref[0]) + sh_ref[0]
    o_ref[...] = jnp.dot(h.astype(BF16), w_ref[...], preferred_element_type=F32).astype(BF16)


def _qkv_proj(x2, sh, sc, g, w_bf, seq):
    t, d = x2.shape
    tm = QKV_ROW_TILE
    per_b = seq // tm
    return pl.pallas_call(
        _qkv_kernel,
        out_shape=jax.ShapeDtypeStruct((t, D_QKV), BF16),
        grid=(D_QKV // QKV_COL_TILE, t // tm),
        in_specs=[pl.BlockSpec((tm, d), lambda j, i: (i, 0)),
                  pl.BlockSpec((1, 1, d), lambda j, i: (i // per_b, 0, 0)),
                  pl.BlockSpec((1, 1, d), lambda j, i: (i // per_b, 0, 0)),
                  pl.BlockSpec((1, d), lambda j, i: (0, 0)),
                  pl.BlockSpec((d, QKV_COL_TILE), lambda j, i: (0, j))],
        out_specs=pl.BlockSpec((tm, QKV_COL_TILE), lambda j, i: (i, j)),
        compiler_params=_cparams(("arbitrary", "arbitrary")),
        name="qkv_proj",
    )(x2, sh, sc, g, w_bf)


def _swa_kernel(sink_ref, q_ref, kvc_ref, kvp_ref, bias_ref, g_ref, o_ref):
    n = pl.program_id(1)
    w = SWA_WINDOW
    tsel = jnp.minimum(n, 1)
    qt = (q_ref[0].astype(F32) * (SCALE * LOG2E)).T.astype(BF16)
    kv = jnp.concatenate([kvp_ref[0], kvc_ref[0]], axis=0)
    k_all = kv[:, :D_SWA_KV]
    vt = kv[:, D_SWA_KV:].astype(F32).T.astype(BF16)
    zeros = jnp.zeros((HEAD_DIM, SWA_GROUP * w), BF16)
    scores = []
    for kh in range(SWA_KV_HEADS):
        h0 = kh * SWA_GROUP
        q_kh = jnp.concatenate([qt[(h0 + g) * HEAD_DIM:(h0 + g + 1) * HEAD_DIM, :]
                                for g in range(SWA_GROUP)], axis=1)
        rhs = jnp.concatenate([q_kh if j == kh else zeros for j in range(SWA_KV_HEADS)], axis=0)
        scores.append(jnp.dot(k_all, rhs, preferred_element_type=F32) + bias_ref[kh, tsel])
    outs = []
    for kh in range(SWA_KV_HEADS):
        h0 = kh * SWA_GROUP
        st = scores[kh]
        sink = jnp.concatenate([jnp.full((1, w), sink_ref[h0 + g] * LOG2E, F32)
                                for g in range(SWA_GROUP)], axis=1)
        m = jnp.maximum(jnp.max(st, axis=0, keepdims=True), sink)
        e = jnp.exp2(st - m)
        denom = jnp.sum(e, axis=0, keepdims=True) + jnp.exp2(sink - m)
        pv = jnp.dot(vt[kh * HEAD_DIM:(kh + 1) * HEAD_DIM, :], e.astype(BF16),
                     preferred_element_type=F32)
        o = pv * (1.0 / denom)
        outs.extend(o[:, g * w:(g + 1) * w] for g in range(SWA_GROUP))
    y = jnp.concatenate(outs, axis=0).T
    o_ref[0] = ((y * _rms(y)) * g_ref[...]).astype(BF16)


def _swa_attention(qkv3, sinks, bias_tab, out_g):
    b, s, _ = qkv3.shape
    w = SWA_WINDOW
    kv_blk = D_SWA // (2 * D_SWA_KV)
    return pl.pallas_call(
        _swa_kernel,
        out_shape=jax.ShapeDtypeStruct((b, s, D_SWA), BF16),
        grid=(b, s // w),
        in_specs=[pl.BlockSpec(memory_space=pltpu.SMEM),
                  pl.BlockSpec((1, w, D_SWA), lambda bi, n: (bi, n, 0)),
                  pl.BlockSpec((1, w, 2 * D_SWA_KV), lambda bi, n: (bi, n, kv_blk)),
                  pl.BlockSpec((1, w, 2 * D_SWA_KV), lambda bi, n: (bi, jnp.maximum(n - 1, 0), kv_blk)),
                  pl.BlockSpec((SWA_KV_HEADS, 2, 2 * w, SWA_GROUP * w), lambda bi, n: (0, 0, 0, 0)),
                  pl.BlockSpec((1, D_SWA), lambda bi, n: (0, 0))],
        out_specs=pl.BlockSpec((1, w, D_SWA), lambda bi, n: (bi, n, 0)),
        compiler_params=_cparams(("arbitrary", "arbitrary")),
        name="swa_attention",
    )(sinks, qkv3, qkv3, qkv3, bias_tab, out_g)


def _moba_kernel(q_ref, k_ref, v_ref, bias_ref, o_ref,
                 vt_sc, kmean_sc, rhs_sc, sel_sc, s_sc, p_sc, mb_sc, al_sc, al2_sc, m_sc, acc_sc, *, nblk):
    c = pl.program_id(2)
    blk = MOBA_BLOCK
    nh = MOBA_STEP_HEADS
    far = bias_ref.shape[1] - 1

    @pl.when(c == 0)
    def _prep():
        def body(j, carry):
            off = pl.multiple_of(j * blk, blk)
            vt = v_ref[0, pl.ds(off, blk), :].astype(F32).T.astype(BF16)
            ones = jnp.ones((BF16_ROWS, blk), BF16)
            parts = []
            for hh in range(nh):
                parts += [vt[hh * HEAD_DIM:(hh + 1) * HEAD_DIM], ones]
            vt_sc[j] = jnp.concatenate(parts, axis=0)
            kb = k_ref[0, pl.ds(off, blk), :].astype(F32)
            kmean_sc[pl.ds(j, 1), :] = jnp.sum(kb, axis=0, keepdims=True) * (1.0 / blk)
            return carry
        lax.fori_loop(0, nblk, body, 0)

    qt = q_ref[0].astype(F32).T
    row = lax.broadcasted_iota(I32, (nh * HEAD_DIM, blk), 0)
    blk_i = lax.broadcasted_iota(I32, (nblk, blk), 0)
    for hh in range(nh):
        in_head = jnp.where(row >= hh * HEAD_DIM, jnp.where(row < (hh + 1) * HEAD_DIM, 1, 0), 0) > 0
        qt_h = jnp.where(in_head, qt, 0.0)
        rhs_sc[hh] = (qt_h * (SCALE * LOG2E)).astype(BF16)
        gate = jnp.dot(kmean_sc[...], qt_h, precision=HIGHEST, preferred_element_type=F32)
        gate = jnp.where(blk_i < c, gate, NEG)
        sel = jnp.where(blk_i == c, 1.0, 0.0)
        for _ in range(min(MOBA_TOPK, nblk)):
            mx = jnp.max(gate, axis=0, keepdims=True)
            idx = jnp.min(jnp.where(gate == mx, blk_i, nblk), axis=0, keepdims=True)
            hit = blk_i == idx
            sel = jnp.where(hit, jnp.where(blk_i < c, 1.0, sel), sel)
            gate = jnp.where(hit, -jnp.inf, gate)
        sel_sc[hh] = sel
        m_sc[hh] = jnp.full((1, blk), NEG, F32)
        acc_sc[hh] = jnp.zeros((V_ROWS, blk), F32)

    def stage_a(u):
        m = [m_sc[hh] for hh in range(nh)]
        for ti in range(2):
            jt = 2 * u + ti
            off = pl.multiple_of(jt * blk, blk)
            kj = k_ref[0, pl.ds(off, blk), :]
            d = jnp.clip(c - jt, 0, far)
            for hh in range(nh):
                st = jnp.dot(kj, rhs_sc[hh], preferred_element_type=F32) + bias_ref[hh, d]
                s_sc[ti, hh] = st
                chosen = sel_sc[hh, pl.ds(jt, 1), :] > 0.5
                m_big = jnp.maximum(m[hh], jnp.max(st, axis=0, keepdims=True))
                m_new = jnp.where(chosen, m_big, m[hh])
                mb_sc[ti, hh] = m_big
                al_sc[ti, hh] = jnp.exp2(m[hh] - m_new)
                m[hh] = m_new
        for hh in range(nh):
            m_sc[hh] = m[hh]

    def stage_b_exp():
        for ti in range(2):
            for hh in range(nh):
                p_sc[ti, hh] = jnp.exp2(s_sc[ti, hh] - mb_sc[ti, hh]).astype(BF16)

    def stage_b_pv(u):
        for ti in range(2):
            jt = 2 * u + ti
            for hh in range(nh):
                chosen = sel_sc[hh, pl.ds(jt, 1), :] > 0.5
                pv = jnp.dot(vt_sc[jt, pl.ds(hh * V_ROWS, V_ROWS), :], p_sc[ti, hh],
                             preferred_element_type=F32)
                acc_sc[hh] = al2_sc[ti, hh] * acc_sc[hh] + jnp.where(chosen, pv, 0.0)

    n_pairs = (c + 2) // 2
    stage_a(0)

    def body(u, carry):
        stage_b_exp()
        al2_sc[...] = al_sc[...]
        stage_a(u)
        stage_b_pv(u - 1)
        return carry
    lax.fori_loop(1, n_pairs, body, 0)
    stage_b_exp()
    al2_sc[...] = al_sc[...]
    stage_b_pv(n_pairs - 1)

    outs = []
    for hh in range(nh):
        inv_l = 1.0 / acc_sc[hh, HEAD_DIM:HEAD_DIM + 1, :]
        outs.append(acc_sc[hh, :HEAD_DIM, :] * inv_l)
    o_ref[0] = jnp.concatenate(outs, axis=0).T


def _moba_attention(qkv3, bias_tab):
    b, s, _ = qkv3.shape
    blk = MOBA_BLOCK
    nblk = s // blk
    assert nblk % 2 == 0
    nh = MOBA_STEP_HEADS
    groups = MOBA_HEADS // nh
    pw = nh * HEAD_DIM
    q0 = (D_SWA + 2 * D_SWA_KV) // pw
    k0 = q0 + D_MOBA // pw
    v0 = k0 + D_MOBA // pw
    nt = bias_tab.shape[1]
    return pl.pallas_call(
        functools.partial(_moba_kernel, nblk=nblk),
        out_shape=jax.ShapeDtypeStruct((b, s, D_MOBA), F32),
        grid=(b, groups, nblk),
        in_specs=[pl.BlockSpec((1, blk, pw), lambda bi, hp, c: (bi, c, q0 + hp)),
                  pl.BlockSpec((1, s, pw), lambda bi, hp, c: (bi, 0, k0 + hp)),
                  pl.BlockSpec((1, s, pw), lambda bi, hp, c: (bi, 0, v0 + hp)),
                  pl.BlockSpec((nh, nt, blk, blk), lambda bi, hp, c: (hp, 0, 0, 0))],
        out_specs=pl.BlockSpec((1, blk, pw), lambda bi, hp, c: (bi, c, hp)),
        scratch_shapes=[pltpu.VMEM((nblk, nh * V_ROWS, blk), BF16),
                        pltpu.VMEM((nblk, pw), F32),
                        pltpu.VMEM((nh, pw, blk), BF16),
                        pltpu.VMEM((nh, nblk, blk), F32),
                        pltpu.VMEM((2, nh, blk, blk), F32),
                        pltpu.VMEM((2, nh, blk, blk), BF16),
                        pltpu.VMEM((2, nh, 1, blk), F32),
                        pltpu.VMEM((2, nh, 1, blk), F32),
                        pltpu.VMEM((2, nh, 1, blk), F32),
                        pltpu.VMEM((nh, 1, blk), F32),
                        pltpu.VMEM((nh, V_ROWS, blk), F32)],
        compiler_params=_cparams(("arbitrary", "arbitrary", "arbitrary")),
        name="moba_attention",
    )(qkv3, qkv3, qkv3, bias_tab)


def _oproj_kernel(ya_ref, yb_ref, x_ref, gb_ref, wo_ref, g1_ref, n2g_ref, sc2_ref, sh2_ref,
                  wr2_ref, br_ref, xo_ref, h2_ref, lg_ref):
    yb = yb_ref[...]
    ybn = ((yb * _rms(yb)) * gb_ref[...]).astype(BF16)
    y = (jnp.dot(ya_ref[...], wo_ref[0:D_SWA, :], preferred_element_type=F32)
         + jnp.dot(ybn, wo_ref[D_SWA:D_SWA + D_MOBA, :], preferred_element_type=F32))
    xn = x_ref[...] + g1_ref[0] * y
    xo_ref[...] = xn
    h2 = (xn * _rms(xn)) * n2g_ref[...] * (1.0 + sc2_ref[0]) + sh2_ref[0]
    h2_ref[...] = h2
    hi = h2.astype(BF16)
    lo = (h2 - hi.astype(F32)).astype(BF16)
    both = jnp.dot(hi, wr2_ref[...], preferred_element_type=F32)
    lg_ref[...] = (both[:, :LANES] + both[:, LANES:]
                   + jnp.dot(lo, wr2_ref[:, :LANES], preferred_element_type=F32) + br_ref[...])


def _oproj(ya, yb, x2, gb, wo_bf, g1, n2g, sc2, sh2, wr2, br, seq):
    t, d = x2.shape
    tm = OPROJ_ROW_TILE
    per_b = seq // tm
    row = lambda i: (i, 0)
    fix = lambda i: (0, 0)
    bat = lambda i: (i // per_b, 0, 0)
    return pl.pallas_call(
        _oproj_kernel,
        out_shape=(jax.ShapeDtypeStruct((t, d), F32), jax.ShapeDtypeStruct((t, d), F32),
                   jax.ShapeDtypeStruct((t, LANES), F32)),
        grid=(t // tm,),
        in_specs=[pl.BlockSpec((tm, D_SWA), row), pl.BlockSpec((tm, D_MOBA), row),
                  pl.BlockSpec((tm, d), row), pl.BlockSpec((1, D_MOBA), fix),
                  pl.BlockSpec((D_SWA + D_MOBA, d), fix), pl.BlockSpec((1, 1, d), bat),
                  pl.BlockSpec((1, d), fix), pl.BlockSpec((1, 1, d), bat), pl.BlockSpec((1, 1, d), bat),
                  pl.BlockSpec((d, 2 * LANES), fix), pl.BlockSpec((1, LANES), fix)],
        out_specs=(pl.BlockSpec((tm, d), row), pl.BlockSpec((tm, d), row), pl.BlockSpec((tm, LANES), row)),
        compiler_params=_cparams(("arbitrary",)),
        name="oproj",
    )(ya, yb, x2, gb, wo_bf, g1, n2g, sc2, sh2, wr2, br)


def _route_kernel(lg_ref, ri_ref, rf_ref, cnt_ref, carry_sc):
    i = pl.program_id(0)

    @pl.when(i == 0)
    def _init():
        carry_sc[...] = jnp.zeros_like(carry_sc)

    lg = lg_ref[...]
    tm = lg.shape[0]
    lane = lax.broadcasted_iota(I32, (tm, LANES), 1)
    ninf = -jnp.inf
    gl = jnp.where(lane < N_GROUPS, lg, ninf)
    ge = jnp.exp(gl - jnp.max(gl, axis=-1, keepdims=True))
    gp = ge / jnp.sum(ge, axis=-1, keepdims=True)
    p_grp = jnp.max(gp, axis=-1, keepdims=True)
    g_idx = jnp.min(jnp.where(gp == p_grp, lane, LANES), axis=-1, keepdims=True)
    lo = N_GROUPS + g_idx * EXPERTS_PER_GROUP
    in_grp = jnp.where(lane >= lo, jnp.where(lane < lo + EXPERTS_PER_GROUP, 1, 0), 0) > 0
    el = jnp.where(in_grp, lg, ninf)
    v0 = jnp.max(el, axis=-1, keepdims=True)
    i0 = jnp.min(jnp.where(el == v0, lane, LANES), axis=-1, keepdims=True)
    el = jnp.where(lane == i0, ninf, el)
    v1 = jnp.max(el, axis=-1, keepdims=True)
    i1 = jnp.min(jnp.where(el == v1, lane, LANES), axis=-1, keepdims=True)
    e0 = i0 - N_GROUPS
    e1 = i1 - N_GROUPS
    ex1 = jnp.exp(v1 - v0)
    den = 1.0 + ex1
    w0 = p_grp * (1.0 / den)
    w1 = p_grp * (ex1 / den)
    onehot = jnp.where(lane == e0, 1.0, jnp.where(lane == e1, 1.0, 0.0))
    r_i = lax.broadcasted_iota(I32, (tm, tm), 0)
    c_i = lax.broadcasted_iota(I32, (tm, tm), 1)
    tri = jnp.where(r_i > c_i, 1.0, 0.0).astype(BF16)
    cum = jnp.dot(tri, onehot.astype(BF16), preferred_element_type=F32) + carry_sc[...]
    pos0 = jnp.sum(jnp.where(lane == e0, cum, 0.0), axis=-1, keepdims=True).astype(I32)
    pos1 = jnp.sum(jnp.where(lane == e1, cum, 0.0), axis=-1, keepdims=True).astype(I32)
    carry_sc[...] = carry_sc[...] + jnp.sum(onehot, axis=0, keepdims=True)
    cnt_ref[...] = carry_sc[...]
    ri_ref[...] = jnp.where(lane == 0, e0, jnp.where(lane == 1, e1,
                            jnp.where(lane == 2, pos0, jnp.where(lane == 3, pos1, 0))))
    rf_ref[...] = jnp.where(lane == 0, w0, jnp.where(lane == 1, w1, 0.0))


def _route(logits):
    t = logits.shape[0]
    tm = ROUTE_ROW_TILE
    return pl.pallas_call(
        _route_kernel,
        out_shape=(jax.ShapeDtypeStruct((t, LANES), I32), jax.ShapeDtypeStruct((t, LANES), F32),
                   jax.ShapeDtypeStruct((1, LANES), F32)),
        grid=(t // tm,),
        in_specs=[pl.BlockSpec((tm, LANES), lambda i: (i, 0))],
        out_specs=(pl.BlockSpec((tm, LANES), lambda i: (i, 0)), pl.BlockSpec((tm, LANES), lambda i: (i, 0)),
                   pl.BlockSpec((1, LANES), lambda i: (0, 0))),
        scratch_shapes=[pltpu.VMEM((1, LANES), F32)],
        compiler_params=_cparams(("arbitrary",)),
        name="route",
    )(logits)


def _row_copy(src, dst, sem):
    return pltpu.make_async_copy(src, dst, sem)


def _dispatch_kernel(d0_ref, d1_ref, h_ref, xin_ref, xg_ref, sem):
    del xin_ref
    tm = h_ref.shape[0]
    base = pl.program_id(0) * tm

    def issue(r, carry):
        src = h_ref.at[pl.ds(r, 1), :]
        _row_copy(src, xg_ref.at[pl.ds(d0_ref[base + r], 1), :], sem.at[0]).start(priority=0)
        _row_copy(src, xg_ref.at[pl.ds(d1_ref[base + r], 1), :], sem.at[0]).start(priority=1)
        return carry
    lax.fori_loop(0, tm, issue, 0, unroll=DMA_UNROLL)

    def drain(r, carry):
        src = h_ref.at[pl.ds(r, 1), :]
        _row_copy(src, xg_ref.at[pl.ds(d0_ref[base + r], 1), :], sem.at[0]).wait()
        _row_copy(src, xg_ref.at[pl.ds(d1_ref[base + r], 1), :], sem.at[0]).wait()
        return carry
    lax.fori_loop(0, tm, drain, 0, unroll=DMA_UNROLL)


def _dispatch(d0, d1, h2, xg):
    t, d = h2.shape
    tm = MOVE_ROW_TILE
    return pl.pallas_call(
        _dispatch_kernel,
        out_shape=jax.ShapeDtypeStruct(xg.shape, xg.dtype),
        grid_spec=pltpu.PrefetchScalarGridSpec(
            num_scalar_prefetch=2, grid=(t // tm,),
            in_specs=[pl.BlockSpec((tm, d), lambda i, a, b: (i, 0)),
                      pl.BlockSpec(memory_space=pl.ANY)],
            out_specs=pl.BlockSpec(memory_space=pl.ANY),
            scratch_shapes=[pltpu.SemaphoreType.DMA((1,))]),
        input_output_aliases={3: 0},
        compiler_params=_cparams(("arbitrary",)),
        name="dispatch",
    )(d0, d1, h2, xg)


def _ffn_kernel(te_ref, nu_ref, x_ref, wg_ref, wu_ref, wd_ref, y_ref, wg_sc, wu_sc, wd_sc):
    i = pl.program_id(0)
    new_expert = jnp.logical_or(i == 0, te_ref[i] != te_ref[jnp.maximum(i - 1, 0)])

    @pl.when(new_expert)
    def _cast():
        wg_sc[...] = wg_ref[0].astype(BF16)
        wu_sc[...] = wu_ref[0].astype(BF16)
        wd_sc[...] = wd_ref[0].astype(BF16)

    @pl.when(i < nu_ref[0])
    def _compute():
        xb = x_ref[...].astype(BF16)
        g = jnp.dot(xb, wg_sc[...], preferred_element_type=F32)
        u = jnp.dot(xb, wu_sc[...], preferred_element_type=F32)
        hid = (_silu(g) * u).astype(BF16)
        y_ref[...] = jnp.dot(hid, wd_sc[...], preferred_element_type=F32)

    @pl.when(i >= nu_ref[0])
    def _skip():
        y_ref[...] = jnp.zeros_like(y_ref)


def _expert_ffn(tile_expert, n_used, xg, wg, wu, wd, layer):
    npad, d = xg.shape
    tr = EXPERT_ROW_TILE
    e_base = layer * N_EXPERTS
    wmap = lambda i, te, nu: (e_base + te[i], 0, 0)
    return pl.pallas_call(
        _ffn_kernel,
        out_shape=jax.ShapeDtypeStruct((npad, d), F32),
        grid_spec=pltpu.PrefetchScalarGridSpec(
            num_scalar_prefetch=2, grid=(npad // tr,),
            in_specs=[pl.BlockSpec((tr, d), lambda i, te, nu: (i, 0)),
                      pl.BlockSpec((1, d, D_EXPERT), wmap),
                      pl.BlockSpec((1, d, D_EXPERT), wmap),
                      pl.BlockSpec((1, D_EXPERT, d), wmap)],
            out_specs=pl.BlockSpec((tr, d), lambda i, te, nu: (i, 0)),
            scratch_shapes=[pltpu.VMEM((d, D_EXPERT), BF16), pltpu.VMEM((d, D_EXPERT), BF16),
                            pltpu.VMEM((D_EXPERT, d), BF16)]),
        compiler_params=_cparams(("arbitrary",)),
        name="expert_ffn",
    )(tile_expert, n_used, xg, wg, wu, wd)


def _combine_kernel(d0_ref, d1_ref, x_ref, rf_ref, g2_ref, fg_ref, y_ref, o_ref, buf, sem, *, final):
    tm = x_ref.shape[0]
    base = pl.program_id(0) * tm

    def issue(r, carry):
        _row_copy(y_ref.at[pl.ds(d0_ref[base + r], 1), :], buf.at[0, pl.ds(r, 1), :], sem.at[0]).start(priority=0)
        _row_copy(y_ref.at[pl.ds(d1_ref[base + r], 1), :], buf.at[1, pl.ds(r, 1), :], sem.at[0]).start(priority=1)
        return carry
    lax.fori_loop(0, tm, issue, 0, unroll=DMA_UNROLL)

    def drain(r, carry):
        _row_copy(y_ref.at[pl.ds(d0_ref[base + r], 1), :], buf.at[0, pl.ds(r, 1), :], sem.at[0]).wait()
        _row_copy(y_ref.at[pl.ds(d1_ref[base + r], 1), :], buf.at[1, pl.ds(r, 1), :], sem.at[0]).wait()
        return carry
    lax.fori_loop(0, tm, drain, 0, unroll=DMA_UNROLL)

    w0 = rf_ref[:, 0:1]
    w1 = rf_ref[:, 1:2]
    xn = x_ref[...] + g2_ref[0] * (w0 * buf[0] + w1 * buf[1])
    if final:
        xn = (xn * _rms(xn)) * fg_ref[...]
    o_ref[...] = xn


def _combine(d0, d1, x2, rf, g2, final_g, y, seq, final):
    t, d = x2.shape
    tm = MOVE_ROW_TILE
    per_b = seq // tm
    return pl.pallas_call(
        functools.partial(_combine_kernel, final=final),
        out_shape=jax.ShapeDtypeStruct((t, d), F32),
        grid_spec=pltpu.PrefetchScalarGridSpec(
            num_scalar_prefetch=2, grid=(t // tm,),
            in_specs=[pl.BlockSpec((tm, d), lambda i, a, b: (i, 0)),
                      pl.BlockSpec((tm, LANES), lambda i, a, b: (i, 0)),
                      pl.BlockSpec((1, 1, d), lambda i, a, b: (i // per_b, 0, 0)),
                      pl.BlockSpec((1, d), lambda i, a, b: (0, 0)),
                      pl.BlockSpec(memory_space=pl.ANY)],
            out_specs=pl.BlockSpec((tm, d), lambda i, a, b: (i, 0)),
            scratch_shapes=[pltpu.VMEM((2, tm, d), F32), pltpu.SemaphoreType.DMA((1,))]),
        compiler_params=_cparams(("arbitrary",)),
        name="combine",
    )(d0, d1, x2, rf, g2, final_g, y)


def _routing_plan(ri, cnt, n_tiles):
    tr = EXPERT_ROW_TILE
    counts = cnt[0, :N_EXPERTS].astype(I32)
    tiles_e = (counts + tr - 1) // tr
    tile_end = jnp.cumsum(tiles_e)
    row_off = (tile_end - tiles_e) * tr
    d0 = jnp.take(row_off, ri[:, 0]) + ri[:, 2]
    d1 = jnp.take(row_off, ri[:, 1]) + ri[:, 3]
    n_used = tile_end[-1]
    tid = jnp.minimum(jnp.arange(n_tiles, dtype=I32), n_used - 1)
    tile_expert = jnp.sum((tid[:, None] >= tile_end[None, :]).astype(I32), axis=1)
    return d0, d1, tile_expert.astype(I32), n_used.reshape(1).astype(I32)


def kernel(x, c, rel_bias, w_ada, b_ada, norm1_g, norm2_g, w_qkv, attn_sinks, swa_out_g, moba_out_g,
           w_o, w_group, b_group, w_expert_router, b_expert_router, w_gate, w_up, w_down, final_g):
    b, s, d = x.shape
    depth = w_ada.shape[0]
    t = b * s
    nblk = s // MOBA_BLOCK

    w = SWA_WINDOW
    key = jnp.arange(2 * w, dtype=I32)[:, None]
    swa_dist = (w + jnp.arange(w, dtype=I32))[None, :] - key
    band = (swa_dist >= 0) & (swa_dist < SWA_WINDOW)
    first = jnp.stack([band & (key >= w), band])
    swa_bkt = jnp.where(first, _rel_bucket(swa_dist)[None], REL_BUCKETS)
    n_tab = min(nblk, FAR_TILE + 1)
    pos = jnp.arange(MOBA_BLOCK, dtype=I32)
    moba_dist = (jnp.arange(n_tab, dtype=I32)[:, None, None] * MOBA_BLOCK
                 + pos[None, None, :] - pos[None, :, None])
    moba_bkt = jnp.where(moba_dist >= 0, _rel_bucket(moba_dist), REL_BUCKETS)
    swa_tab = _bias_table(rel_bias, swa_bkt, SWA_Q_HEADS, 0, scale=LOG2E, group=SWA_GROUP)
    moba_tab = _bias_table(rel_bias, moba_bkt, MOBA_HEADS, SWA_Q_HEADS, scale=LOG2E)

    c_pad = jnp.zeros((8, d), F32).at[:b].set(c)
    mod = _ada_mod(c_pad, w_ada, b_ada)[:, :b]

    w_qkv_bf = w_qkv.astype(BF16)
    w_o_bf = w_o.astype(BF16)
    wg_all = w_gate.reshape(depth * N_EXPERTS, d, D_EXPERT)
    wu_all = w_up.reshape(depth * N_EXPERTS, d, D_EXPERT)
    wd_all = w_down.reshape(depth * N_EXPERTS, D_EXPERT, d)

    n_tiles = (t * 2) // EXPERT_ROW_TILE + N_EXPERTS
    xg = jnp.zeros((n_tiles * EXPERT_ROW_TILE, d), F32)
    x2 = x.reshape(t, d)
    fg = final_g.reshape(1, d)
    for l in range(depth):
        sh1, sc1, g1, sh2, sc2, g2 = [mod[l, :, k * d:(k + 1) * d].reshape(b, 1, d) for k in range(6)]
        qkv = _qkv_proj(x2, sh1, sc1, norm1_g[l].reshape(1, d), w_qkv_bf[l], s)
        qkv3 = qkv.reshape(b, s, D_QKV)
        ya = _swa_attention(qkv3, attn_sinks[l], swa_tab, swa_out_g[l].reshape(1, D_SWA))
        yb = _moba_attention(qkv3, moba_tab)
        wr = jnp.zeros((d, LANES), F32).at[:, :N_GROUPS].set(w_group[l])
        wr = wr.at[:, N_GROUPS:N_GROUPS + N_EXPERTS].set(w_expert_router[l])
        wr_hi = wr.astype(BF16)
        wr_lo = (wr - wr_hi.astype(F32)).astype(BF16)
        wr2 = jnp.concatenate([wr_hi, wr_lo], axis=1)
        br = jnp.zeros((1, LANES), F32).at[0, :N_GROUPS].set(b_group[l])
        br = br.at[0, N_GROUPS:N_GROUPS + N_EXPERTS].set(b_expert_router[l])
        x2, h2, logits = _oproj(ya.reshape(t, D_SWA), yb.reshape(t, D_MOBA), x2,
                                moba_out_g[l].reshape(1, D_MOBA), w_o_bf[l], g1,
                                norm2_g[l].reshape(1, d), sc2, sh2, wr2, br, s)
        ri, rf, cnt = _route(logits)
        d0, d1, tile_expert, n_used = _routing_plan(ri, cnt, n_tiles)
        xg = _dispatch(d0, d1, h2, xg)
        y = _expert_ffn(tile_expert, n_used, xg, wg_all, wu_all, wd_all, l)
        x2 = _combine(d0, d1, x2, rf, g2, fg, y, s, final=(l == depth - 1))
    return x2.reshape(b, s, d)
```

```python
import functools
import math

import jax
import jax.numpy as jnp
from jax import lax
from jax.experimental import pallas as pl
from jax.experimental.pallas import tpu as pltpu

F32 = jnp.float32
BF16 = jnp.bfloat16
I32 = jnp.int32
HIGHEST = lax.Precision.HIGHEST

D_MODEL = 2048
HEAD_DIM = 64
SWA_Q_HEADS = 16
SWA_KV_HEADS = 2
SWA_GROUP = SWA_Q_HEADS // SWA_KV_HEADS
SWA_WINDOW = 128
MOBA_HEADS = 16
MOBA_BLOCK = 256
MOBA_TOPK = 3
D_SWA = SWA_Q_HEADS * HEAD_DIM
D_SWA_KV = SWA_KV_HEADS * HEAD_DIM
D_MOBA = MOBA_HEADS * HEAD_DIM
D_QKV = D_SWA + 2 * D_SWA_KV + 3 * D_MOBA
REL_BUCKETS = 32
REL_MAX_DISTANCE = 2048
N_GROUPS = 4
EXPERTS_PER_GROUP = 8
N_EXPERTS = N_GROUPS * EXPERTS_PER_GROUP
D_EXPERT = 512
EPS = 1e-6
NEG = -1e30
SCALE = 1.0 / math.sqrt(HEAD_DIM)
LOG2E = math.log2(math.e)

LANES = 128
BF16_ROWS = 16
V_ROWS = HEAD_DIM + BF16_ROWS
MOBA_STEP_HEADS = 4
FAR_TILE = REL_MAX_DISTANCE // MOBA_BLOCK + 1
QKV_COL_TILE = D_QKV // 2
QKV_ROW_TILE = 1024
OPROJ_ROW_TILE = 512
ROUTE_ROW_TILE = 512
MOVE_ROW_TILE = 256
EXPERT_ROW_TILE = 256
ADA_COL_TILE = 1024
DMA_UNROLL = 8
VMEM_LIMIT = 56 * 1024 * 1024


def _cparams(sem, vmem=VMEM_LIMIT):
    return pltpu.CompilerParams(dimension_semantics=sem, vmem_limit_bytes=vmem)


def _rel_bucket(dist):
    n = jnp.maximum(dist, 0)
    max_exact = REL_BUCKETS // 2
    nf = jnp.maximum(n, 1).astype(F32)
    large = max_exact + (jnp.log(nf / max_exact) / math.log(REL_MAX_DISTANCE / max_exact)
                         * (REL_BUCKETS - max_exact)).astype(I32)
    large = jnp.minimum(large, REL_BUCKETS - 1)
    return jnp.where(n < max_exact, n, large)


def _silu(v):
    return v * (1.0 / (1.0 + jnp.exp(-v)))


def _ada_kernel(c_ref, w_ref, b_ref, o_ref):
    ca = _silu(c_ref[...])
    o_ref[0] = jnp.dot(ca, w_ref[0], precision=HIGHEST, preferred_element_type=F32) + b_ref[0]


def _ada_mod(c_pad, w_ada, b_ada):
    depth, d, n = w_ada.shape
    rows = c_pad.shape[0]
    return pl.pallas_call(
        _ada_kernel,
        out_shape=jax.ShapeDtypeStruct((depth, rows, n), F32),
        grid=(depth, n // ADA_COL_TILE),
        in_specs=[pl.BlockSpec((rows, d), lambda l, j: (0, 0)),
                  pl.BlockSpec((1, d, ADA_COL_TILE), lambda l, j: (l, 0, j)),
                  pl.BlockSpec((1, 1, ADA_COL_TILE), lambda l, j: (l, 0, j))],
        out_specs=pl.BlockSpec((1, rows, ADA_COL_TILE), lambda l, j: (l, 0, j)),
        compiler_params=_cparams(("arbitrary", "arbitrary")),
        name="ada_mod",
    )(c_pad, w_ada, b_ada.reshape(depth, 1, n))


def _bias_kernel(rb_ref, bkt_ref, o_ref, *, head0, scale):
    h = pl.program_id(0) + head0
    b = bkt_ref[0]
    acc = jnp.full(b.shape, NEG, F32)
    for k in range(REL_BUCKETS):
        acc = jnp.where(b == k, rb_ref[k, h] * scale, acc)
    o_ref[0, 0] = acc


def _bias_table(rel_bias, bkt, n_heads, head0, scale=1.0, group=1):
    nt, r, c = bkt.shape
    return pl.pallas_call(
        functools.partial(_bias_kernel, head0=head0, scale=scale),
        out_shape=jax.ShapeDtypeStruct((n_heads // group, nt, r, c * group), F32),
        grid=(n_heads, nt),
        in_specs=[pl.BlockSpec(memory_space=pltpu.SMEM),
                  pl.BlockSpec((1, r, c), lambda h, t: (t, 0, 0))],
        out_specs=pl.BlockSpec((1, 1, r, c), lambda h, t: (h // group, t, 0, h % group)),
        compiler_params=_cparams(("arbitrary", "arbitrary")),
        name="bias_table",
    )(rel_bias, bkt)


def _rms(v):
    return lax.rsqrt(jnp.mean(v * v, axis=-1, keepdims=True) + EPS)


def _qkv_kernel(x_ref, sh_ref, sc_ref, g_ref, w_ref, o_ref):
    x = x_ref[...]
    h = (x * _rms(x)) * g_ref[...] * (1.0 + sc_ref[0]) + sh_ref[0]
    o_ref[...] = jnp.dot(h.astype(BF16), w_ref[...], preferred_element_type=F32).astype(BF16)


def _qkv_proj(x2, sh, sc, g, w_bf, seq):
    t, d = x2.shape
    tm = QKV_ROW_TILE
    per_b = seq // tm
    return pl.pallas_call(
        _qkv_kernel,
        out_shape=jax.ShapeDtypeStruct((t, D_QKV), BF16),
        grid=(D_QKV // QKV_COL_TILE, t // tm),
        in_specs=[pl.BlockSpec((tm, d), lambda j, i: (i, 0)),
                  pl.BlockSpec((1, 1, d), lambda j, i: (i // per_b, 0, 0)),
                  pl.BlockSpec((1, 1, d), lambda j, i: (i // per_b, 0, 0)),
                  pl.BlockSpec((1, d), lambda j, i: (0, 0)),
                  pl.BlockSpec((d, QKV_COL_TILE), lambda j, i: (0, j))],
        out_specs=pl.BlockSpec((tm, QKV_COL_TILE), lambda j, i: (i, j)),
        compiler_params=_cparams(("arbitrary", "arbitrary")),
        name="qkv_proj",
    )(x2, sh, sc, g, w_bf)


def _swa_kernel(sink_ref, q_ref, kvc_ref, kvp_ref, bias_ref, g_ref, o_ref):
    n = pl.program_id(1)
    w = SWA_WINDOW
    tsel = jnp.minimum(n, 1)
    qt = (q_ref[0].astype(F32) * (SCALE * LOG2E)).T.astype(BF16)
    kv = jnp.concatenate([kvp_ref[0], kvc_ref[0]], axis=0)
    k_all = kv[:, :D_SWA_KV]
    vt = kv[:, D_SWA_KV:].astype(F32).T.astype(BF16)
    zeros = jnp.zeros((HEAD_DIM, SWA_GROUP * w), BF16)
    scores = []
    for kh in range(SWA_KV_HEADS):
        h0 = kh * SWA_GROUP
        q_kh = jnp.concatenate([qt[(h0 + g) * HEAD_DIM:(h0 + g + 1) * HEAD_DIM, :]
                                for g in range(SWA_GROUP)], axis=1)
        rhs = jnp.concatenate([q_kh if j == kh else zeros for j in range(SWA_KV_HEADS)], axis=0)
        scores.append(jnp.dot(k_all, rhs, preferred_element_type=F32) + bias_ref[kh, tsel])
    outs = []
    for kh in range(SWA_KV_HEADS):
        h0 = kh * SWA_GROUP
        st = scores[kh]
        sink = jnp.concatenate([jnp.full((1, w), sink_ref[h0 + g] * LOG2E, F32)
                                for g in range(SWA_GROUP)], axis=1)
        m = jnp.maximum(jnp.max(st, axis=0, keepdims=True), sink)
        e = jnp.exp2(st - m)
        denom = jnp.sum(e, axis=0, keepdims=True) + jnp.exp2(sink - m)
        pv = jnp.dot(vt[kh * HEAD_DIM:(kh + 1) * HEAD_DIM, :], e.astype(BF16),
                     preferred_element_type=F32)
        o = pv * (1.0 / denom)
        outs.extend(o[:, g * w:(g + 1) * w] for g in range(SWA_GROUP))
    y = jnp.concatenate(outs, axis=0).T
    o_ref[0] = ((y * _rms(y)) * g_ref[...]).astype(BF16)


def _swa_attention(qkv3, sinks, bias_tab, out_g):
    b, s, _ = qkv3.shape
    w = SWA_WINDOW
    kv_blk = D_SWA // (2 * D_SWA_KV)
    return pl.pallas_call(
        _swa_kernel,
        out_shape=jax.ShapeDtypeStruct((b, s, D_SWA), BF16),
        grid=(b, s // w),
        in_specs=[pl.BlockSpec(memory_space=pltpu.SMEM),
                  pl.BlockSpec((1, w, D_SWA), lambda bi, n: (bi, n, 0)),
                  pl.BlockSpec((1, w, 2 * D_SWA_KV), lambda bi, n: (bi, n, kv_blk)),
                  pl.BlockSpec((1, w, 2 * D_SWA_KV), lambda bi, n: (bi, jnp.maximum(n - 1, 0), kv_blk)),
                  pl.BlockSpec((SWA_KV_HEADS, 2, 2 * w, SWA_GROUP * w), lambda bi, n: (0, 0, 0, 0)),
                  pl.BlockSpec((1, D_SWA), lambda bi, n: (0, 0))],
        out_specs=pl.BlockSpec((1, w, D_SWA), lambda bi, n: (bi, n, 0)),
        compiler_params=_cparams(("arbitrary", "arbitrary")),
        name="swa_attention",
    )(sinks, qkv3, qkv3, qkv3, bias_tab, out_g)


def _moba_kernel(q_ref, k_ref, v_ref, bias_ref, o_ref,
                 vt_sc, kmean_sc, rhs_sc, sel_sc, s_sc, p_sc, mb_sc, al_sc, al2_sc, m_sc, acc_sc, *, nblk):
    c = pl.program_id(2)
    blk = MOBA_BLOCK
    nh = MOBA_STEP_HEADS
    far = bias_ref.shape[1] - 1

    @pl.when(c == 0)
    def _prep():
        def body(j, carry):
            off = pl.multiple_of(j * blk, blk)
            vt = v_ref[0, pl.ds(off, blk), :].astype(F32).T.astype(BF16)
            ones = jnp.ones((BF16_ROWS, blk), BF16)
            parts = []
            for hh in range(nh):
                parts += [vt[hh * HEAD_DIM:(hh + 1) * HEAD_DIM], ones]
            vt_sc[j] = jnp.concatenate(parts, axis=0)
            kb = k_ref[0, pl.ds(off, blk), :].astype(F32)
            kmean_sc[pl.ds(j, 1), :] = jnp.sum(kb, axis=0, keepdims=True) * (1.0 / blk)
            return carry
        lax.fori_loop(0, nblk, body, 0)

    qt = q_ref[0].astype(F32).T
    row = lax.broadcasted_iota(I32, (nh * HEAD_DIM, blk), 0)
    blk_i = lax.broadcasted_iota(I32, (nblk, blk), 0)
    for hh in range(nh):
        in_head = jnp.where(row >= hh * HEAD_DIM, jnp.where(row < (hh + 1) * HEAD_DIM, 1, 0), 0) > 0
        qt_h = jnp.where(in_head, qt, 0.0)
        rhs_sc[hh] = (qt_h * (SCALE * LOG2E)).astype(BF16)
        gate = jnp.dot(kmean_sc[...], qt_h, precision=HIGHEST, preferred_element_type=F32)
        gate = jnp.where(blk_i < c, gate, NEG)
        sel = jnp.where(blk_i == c, 1.0, 0.0)
        for _ in range(min(MOBA_TOPK, nblk)):
            mx = jnp.max(gate, axis=0, keepdims=True)
            idx = jnp.min(jnp.where(gate == mx, blk_i, nblk), axis=0, keepdims=True)
            hit = blk_i == idx
            sel = jnp.where(hit, jnp.where(blk_i < c, 1.0, sel), sel)
            gate = jnp.where(hit, -jnp.inf, gate)
        sel_sc[hh] = sel
        m_sc[hh] = jnp.full((1, blk), NEG, F32)
        acc_sc[hh] = jnp.zeros((V_ROWS, blk), F32)

    def stage_a(u):
        m = [m_sc[hh] for hh in range(nh)]
        for ti in range(2):
            jt = 2 * u + ti
            off = pl.multiple_of(jt * blk, blk)
            kj = k_ref[0, pl.ds(off, blk), :]
            d = jnp.clip(c - jt, 0, far)
            for hh in range(nh):
                st = jnp.dot(kj, rhs_sc[hh], preferred_element_type=F32) + bias_ref[hh, d]
                s_sc[ti, hh] = st
                chosen = sel_sc[hh, pl.ds(jt, 1), :] > 0.5
                m_big = jnp.maximum(m[hh], jnp.max(st, axis=0, keepdims=True))
                m_new = jnp.where(chosen, m_big, m[hh])
                mb_sc[ti, hh] = m_big
                al_sc[ti, hh] = jnp.exp2(m[hh] - m_new)
                m[hh] = m_new
        for hh in range(nh):
            m_sc[hh] = m[hh]

    def stage_b_exp():
        for ti in range(2):
            for hh in range(nh):
                p_sc[ti, hh] = jnp.exp2(s_sc[ti, hh] - mb_sc[ti, hh]).astype(BF16)

    def stage_b_pv(u):
        for ti in range(2):
            jt = 2 * u + ti
            for hh in range(nh):
                chosen = sel_sc[hh, pl.ds(jt, 1), :] > 0.5
                pv = jnp.dot(vt_sc[jt, pl.ds(hh * V_ROWS, V_ROWS), :], p_sc[ti, hh],
                             preferred_element_type=F32)
                acc_sc[hh] = al2_sc[ti, hh] * acc_sc[hh] + jnp.where(chosen, pv, 0.0)

    n_pairs = (c + 2) // 2
    stage_a(0)

    def body(u, carry):
        stage_b_exp()
        al2_sc[...] = al_sc[...]
        stage_a(u)
        stage_b_pv(u - 1)
        return carry
    lax.fori_loop(1, n_pairs, body, 0)
    stage_b_exp()
    al2_sc[...] = al_sc[...]
    stage_b_pv(n_pairs - 1)

    outs = []
    for hh in range(nh):
        inv_l = 1.0 / acc_sc[hh, HEAD_DIM:HEAD_DIM + 1, :]
        outs.append(acc_sc[hh, :HEAD_DIM, :] * inv_l)
    o_ref[0] = jnp.concatenate(outs, axis=0).T


def _moba_attention(qkv3, bias_tab):
    b, s, _ = qkv3.shape
    blk = MOBA_BLOCK
    nblk = s // blk
    assert nblk % 2 == 0
    nh = MOBA_STEP_HEADS
    groups = MOBA_HEADS // nh
    pw = nh * HEAD_DIM
    q0 = (D_SWA + 2 * D_SWA_KV) // pw
    k0 = q0 + D_MOBA // pw
    v0 = k0 + D_MOBA // pw
    nt = bias_tab.shape[1]
    return pl.pallas_call(
        functools.partial(_moba_kernel, nblk=nblk),
        out_shape=jax.ShapeDtypeStruct((b, s, D_MOBA), F32),
        grid=(b, groups, nblk),
        in_specs=[pl.BlockSpec((1, blk, pw), lambda bi, hp, c: (bi, c, q0 + hp)),
                  pl.BlockSpec((1, s, pw), lambda bi, hp, c: (bi, 0, k0 + hp)),
                  pl.BlockSpec((1, s, pw), lambda bi, hp, c: (bi, 0, v0 + hp)),
                  pl.BlockSpec((nh, nt, blk, blk), lambda bi, hp, c: (hp, 0, 0, 0))],
        out_specs=pl.BlockSpec((1, blk, pw), lambda bi, hp, c: (bi, c, hp)),
        scratch_shapes=[pltpu.VMEM((nblk, nh * V_ROWS, blk), BF16),
                        pltpu.VMEM((nblk, pw), F32),
                        pltpu.VMEM((nh, pw, blk), BF16),
                        pltpu.VMEM((nh, nblk, blk), F32),
                        pltpu.VMEM((2, nh, blk, blk), F32),
                        pltpu.VMEM((2, nh, blk, blk), BF16),
                        pltpu.VMEM((2, nh, 1, blk), F32),
                        pltpu.VMEM((2, nh, 1, blk), F32),
                        pltpu.VMEM((2, nh, 1, blk), F32),
                        pltpu.VMEM((nh, 1, blk), F32),
                        pltpu.VMEM((nh, V_ROWS, blk), F32)],
        compiler_params=_cparams(("arbitrary", "arbitrary", "arbitrary")),
        name="moba_attention",
    )(qkv3, qkv3, qkv3, bias_tab)


def _oproj_kernel(ya_ref, yb_ref, x_ref, gb_ref, wo_ref, g1_ref, n2g_ref, sc2_ref, sh2_ref,
                  wr2_ref, br_ref, xo_ref, h2_ref, lg_ref):
    yb = yb_ref[...]
    ybn = ((yb * _rms(yb)) * gb_ref[...]).astype(BF16)
    y = (jnp.dot(ya_ref[...], wo_ref[0:D_SWA, :], preferred_element_type=F32)
         + jnp.dot(ybn, wo_ref[D_SWA:D_SWA + D_MOBA, :], preferred_element_type=F32))
    xn = x_ref[...] + g1_ref[0] * y
    xo_ref[...] = xn
    h2 = (xn * _rms(xn)) * n2g_ref[...] * (1.0 + sc2_ref[0]) + sh2_ref[0]
    h2_ref[...] = h2
    hi = h2.astype(BF16)
    lo = (h2 - hi.astype(F32)).astype(BF16)
    both = jnp.dot(hi, wr2_ref[...], preferred_element_type=F32)
    lg_ref[...] = (both[:, :LANES] + both[:, LANES:]
                   + jnp.dot(lo, wr2_ref[:, :LANES], preferred_element_type=F32) + br_ref[...])


def _oproj(ya, yb, x2, gb, wo_bf, g1, n2g, sc2, sh2, wr2, br, seq):
    t, d = x2.shape
    tm = OPROJ_ROW_TILE
    per_b = seq // tm
    row = lambda i: (i, 0)
    fix = lambda i: (0, 0)
    bat = lambda i: (i // per_b, 0, 0)
    return pl.pallas_call(
        _oproj_kernel,
        out_shape=(jax.ShapeDtypeStruct((t, d), F32), jax.ShapeDtypeStruct((t, d), F32),
                   jax.ShapeDtypeStruct((t, LANES), F32)),
        grid=(t // tm,),
        in_specs=[pl.BlockSpec((tm, D_SWA), row), pl.BlockSpec((tm, D_MOBA), row),
                  pl.BlockSpec((tm, d), row), pl.BlockSpec((1, D_MOBA), fix),
                  pl.BlockSpec((D_SWA + D_MOBA, d), fix), pl.BlockSpec((1, 1, d), bat),
                  pl.BlockSpec((1, d), fix), pl.BlockSpec((1, 1, d), bat), pl.BlockSpec((1, 1, d), bat),
                  pl.BlockSpec((d, 2 * LANES), fix), pl.BlockSpec((1, LANES), fix)],
        out_specs=(pl.BlockSpec((tm, d), row), pl.BlockSpec((tm, d), row), pl.BlockSpec((tm, LANES), row)),
        compiler_params=_cparams(("arbitrary",)),
        name="oproj",
    )(ya, yb, x2, gb, wo_bf, g1, n2g, sc2, sh2, wr2, br)


def _route_kernel(lg_ref, ri_ref, rf_ref, cnt_ref, carry_sc):
    i = pl.program_id(0)

    @pl.when(i == 0)
    def _init():
        carry_sc[...] = jnp.zeros_like(carry_sc)

    lg = lg_ref[...]
    tm = lg.shape[0]
    lane = lax.broadcasted_iota(I32, (tm, LANES), 1)
    ninf = -jnp.inf
    gl = jnp.where(lane < N_GROUPS, lg, ninf)
    ge = jnp.exp(gl - jnp.max(gl, axis=-1, keepdims=True))
    gp = ge / jnp.sum(ge, axis=-1, keepdims=True)
    p_grp = jnp.max(gp, axis=-1, keepdims=True)
    g_idx = jnp.min(jnp.where(gp == p_grp, lane, LANES), axis=-1, keepdims=True)
    lo = N_GROUPS + g_idx * EXPERTS_PER_GROUP
    in_grp = jnp.where(lane >= lo, jnp.where(lane < lo + EXPERTS_PER_GROUP, 1, 0), 0) > 0
    el = jnp.where(in_grp, lg, ninf)
    v0 = jnp.max(el, axis=-1, keepdims=True)
    i0 = jnp.min(jnp.where(el == v0, lane, LANES), axis=-1, keepdims=True)
    el = jnp.where(lane == i0, ninf, el)
    v1 = jnp.max(el, axis=-1, keepdims=True)
    i1 = jnp.min(jnp.where(el == v1, lane, LANES), axis=-1, keepdims=True)
    e0 = i0 - N_GROUPS
    e1 = i1 - N_GROUPS
    ex1 = jnp.exp(v1 - v0)
    den = 1.0 + ex1
    w0 = p_grp * (1.0 / den)
    w1 = p_grp * (ex1 / den)
    onehot = jnp.where(lane == e0, 1.0, jnp.where(lane == e1, 1.0, 0.0))
    r_i = lax.broadcasted_iota(I32, (tm, tm), 0)
    c_i = lax.broadcasted_iota(I32, (tm, tm), 1)
    tri = jnp.where(r_i > c_i, 1.0, 0.0).astype(BF16)
    cum = jnp.dot(tri, onehot.astype(BF16), preferred_element_type=F32) + carry_sc[...]
    pos0 = jnp.sum(jnp.where(lane == e0, cum, 0.0), axis=-1, keepdims=True).astype(I32)
    pos1 = jnp.sum(jnp.where(lane == e1, cum, 0.0), axis=-1, keepdims=True).astype(I32)
    carry_sc[...] = carry_sc[...] + jnp.sum(onehot, axis=0, keepdims=True)
    cnt_ref[...] = carry_sc[...]
    ri_ref[...] = jnp.where(lane == 0, e0, jnp.where(lane == 1, e1,
                            jnp.where(lane == 2, pos0, jnp.where(lane == 3, pos1, 0))))
    rf_ref[...] = jnp.where(lane == 0, w0, jnp.where(lane == 1, w1, 0.0))


def _route(logits):
    t = logits.shape[0]
    tm = ROUTE_ROW_TILE
    return pl.pallas_call(
        _route_kernel,
        out_shape=(jax.ShapeDtypeStruct((t, LANES), I32), jax.ShapeDtypeStruct((t, LANES), F32),
                   jax.ShapeDtypeStruct((1, LANES), F32)),
        grid=(t // tm,),
        in_specs=[pl.BlockSpec((tm, LANES), lambda i: (i, 0))],
        out_specs=(pl.BlockSpec((tm, LANES), lambda i: (i, 0)), pl.BlockSpec((tm, LANES), lambda i: (i, 0)),
                   pl.BlockSpec((1, LANES), lambda i: (0, 0))),
        scratch_shapes=[pltpu.VMEM((1, LANES), F32)],
        compiler_params=_cparams(("arbitrary",)),
        name="route",
    )(logits)


def _row_copy(src, dst, sem):
    return pltpu.make_async_copy(src, dst, sem)


def _dispatch_kernel(d0_ref, d1_ref, h_ref, xin_ref, xg_ref, sem):
    del xin_ref
    tm = h_ref.shape[0]
    base = pl.program_id(0) * tm

    def issue(r, carry):
        src = h_ref.at[pl.ds(r, 1), :]
        _row_copy(src, xg_ref.at[pl.ds(d0_ref[base + r], 1), :], sem.at[0]).start(priority=0)
        _row_copy(src, xg_ref.at[pl.ds(d1_ref[base + r], 1), :], sem.at[0]).start(priority=1)
        return carry
    lax.fori_loop(0, tm, issue, 0, unroll=DMA_UNROLL)

    def drain(r, carry):
        src = h_ref.at[pl.ds(r, 1), :]
        _row_copy(src, xg_ref.at[pl.ds(d0_ref[base + r], 1), :], sem.at[0]).wait()
        _row_copy(src, xg_ref.at[pl.ds(d1_ref[base + r], 1), :], sem.at[0]).wait()
        return carry
    lax.fori_loop(0, tm, drain, 0, unroll=DMA_UNROLL)


def _dispatch(d0, d1, h2, xg):
    t, d = h2.shape
    tm = MOVE_ROW_TILE
    return pl.pallas_call(
        _dispatch_kernel,
        out_shape=jax.ShapeDtypeStruct(xg.shape, xg.dtype),
        grid_spec=pltpu.PrefetchScalarGridSpec(
            num_scalar_prefetch=2, grid=(t // tm,),
            in_specs=[pl.BlockSpec((tm, d), lambda i, a, b: (i, 0)),
                      pl.BlockSpec(memory_space=pl.ANY)],
            out_specs=pl.BlockSpec(memory_space=pl.ANY),
            scratch_shapes=[pltpu.SemaphoreType.DMA((1,))]),
        input_output_aliases={3: 0},
        compiler_params=_cparams(("arbitrary",)),
        name="dispatch",
    )(d0, d1, h2, xg)


def _ffn_kernel(te_ref, nu_ref, x_ref, wg_ref, wu_ref, wd_ref, y_ref, wg_sc, wu_sc, wd_sc):
    i = pl.program_id(0)
    new_expert = jnp.logical_or(i == 0, te_ref[i] != te_ref[jnp.maximum(i - 1, 0)])

    @pl.when(new_expert)
    def _cast():
        wg_sc[...] = wg_ref[0].astype(BF16)
        wu_sc[...] = wu_ref[0].astype(BF16)
        wd_sc[...] = wd_ref[0].astype(BF16)

    @pl.when(i < nu_ref[0])
    def _compute():
        xb = x_ref[...].astype(BF16)
        g = jnp.dot(xb, wg_sc[...], preferred_element_type=F32)
        u = jnp.dot(xb, wu_sc[...], preferred_element_type=F32)
        hid = (_silu(g) * u).astype(BF16)
        y_ref[...] = jnp.dot(hid, wd_sc[...], preferred_element_type=F32)

    @pl.when(i >= nu_ref[0])
    def _skip():
        y_ref[...] = jnp.zeros_like(y_ref)


def _expert_ffn(tile_expert, n_used, xg, wg, wu, wd, layer):
    npad, d = xg.shape
    tr = EXPERT_ROW_TILE
    e_base = layer * N_EXPERTS
    wmap = lambda i, te, nu: (e_base + te[i], 0, 0)
    return pl.pallas_call(
        _ffn_kernel,
        out_shape=jax.ShapeDtypeStruct((npad, d), F32),
        grid_spec=pltpu.PrefetchScalarGridSpec(
            num_scalar_prefetch=2, grid=(npad // tr,),
            in_specs=[pl.BlockSpec((tr, d), lambda i, te, nu: (jnp.minimum(i, nu[0] - 1), 0)),
                      pl.BlockSpec((1, d, D_EXPERT), wmap),
                      pl.BlockSpec((1, d, D_EXPERT), wmap),
                      pl.BlockSpec((1, D_EXPERT, d), wmap)],
            out_specs=pl.BlockSpec((tr, d), lambda i, te, nu: (i, 0)),
            scratch_shapes=[pltpu.VMEM((d, D_EXPERT), BF16), pltpu.VMEM((d, D_EXPERT), BF16),
                            pltpu.VMEM((D_EXPERT, d), BF16)]),
        compiler_params=_cparams(("arbitrary",)),
        name="expert_ffn",
    )(tile_expert, n_used, xg, wg, wu, wd)


def _combine_kernel(d0_ref, d1_ref, x_ref, rf_ref, g2_ref, fg_ref, y_ref, o_ref, buf, sem, *, final):
    tm = x_ref.shape[0]
    base = pl.program_id(0) * tm

    def issue(r, carry):
        _row_copy(y_ref.at[pl.ds(d0_ref[base + r], 1), :], buf.at[0, pl.ds(r, 1), :], sem.at[0]).start(priority=0)
        _row_copy(y_ref.at[pl.ds(d1_ref[base + r], 1), :], buf.at[1, pl.ds(r, 1), :], sem.at[0]).start(priority=1)
        return carry
    lax.fori_loop(0, tm, issue, 0, unroll=DMA_UNROLL)

    def drain(r, carry):
        _row_copy(y_ref.at[pl.ds(d0_ref[base + r], 1), :], buf.at[0, pl.ds(r, 1), :], sem.at[0]).wait()
        _row_copy(y_ref.at[pl.ds(d1_ref[base + r], 1), :], buf.at[1, pl.ds(r, 1), :], sem.at[0]).wait()
        return carry
    lax.fori_loop(0, tm, drain, 0, unroll=DMA_UNROLL)

    w0 = rf_ref[:, 0:1]
    w1 = rf_ref[:, 1:2]
    xn = x_ref[...] + g2_ref[0] * (w0 * buf[0] + w1 * buf[1])
    if final:
        xn = (xn * _rms(xn)) * fg_ref[...]
    o_ref[...] = xn


def _combine(d0, d1, x2, rf, g2, final_g, y, seq, final):
    t, d = x2.shape
    tm = MOVE_ROW_TILE
    per_b = seq // tm
    return pl.pallas_call(
        functools.partial(_combine_kernel, final=final),
        out_shape=jax.ShapeDtypeStruct((t, d), F32),
        grid_spec=pltpu.PrefetchScalarGridSpec(
            num_scalar_prefetch=2, grid=(t // tm,),
            in_specs=[pl.BlockSpec((tm, d), lambda i, a, b: (i, 0)),
                      pl.BlockSpec((tm, LANES), lambda i, a, b: (i, 0)),
                      pl.BlockSpec((1, 1, d), lambda i, a, b: (i // per_b, 0, 0)),
                      pl.BlockSpec((1, d), lambda i, a, b: (0, 0)),
                      pl.BlockSpec(memory_space=pl.ANY)],
            out_specs=pl.BlockSpec((tm, d), lambda i, a, b: (i, 0)),
            scratch_shapes=[pltpu.VMEM((2, tm, d), F32), pltpu.SemaphoreType.DMA((1,))]),
        compiler_params=_cparams(("arbitrary",)),
        name="combine",
    )(d0, d1, x2, rf, g2, final_g, y)


def _routing_plan(ri, cnt, n_tiles):
    tr = EXPERT_ROW_TILE
    counts = cnt[0, :N_EXPERTS].astype(I32)
    tiles_e = (counts + tr - 1) // tr
    tile_end = jnp.cumsum(tiles_e)
    row_off = (tile_end - tiles_e) * tr
    d0 = jnp.take(row_off, ri[:, 0]) + ri[:, 2]
    d1 = jnp.take(row_off, ri[:, 1]) + ri[:, 3]
    n_used = tile_end[-1]
    tid = jnp.minimum(jnp.arange(n_tiles, dtype=I32), n_used - 1)
    tile_expert = jnp.sum((tid[:, None] >= tile_end[None, :]).astype(I32), axis=1)
    return d0, d1, tile_expert.astype(I32), n_used.reshape(1).astype(I32)


def kernel(x, c, rel_bias, w_ada, b_ada, norm1_g, norm2_g, w_qkv, attn_sinks, swa_out_g, moba_out_g,
           w_o, w_group, b_group, w_expert_router, b_expert_router, w_gate, w_up, w_down, final_g):
    b, s, d = x.shape
    depth = w_ada.shape[0]
    t = b * s
    nblk = s // MOBA_BLOCK

    w = SWA_WINDOW
    key = jnp.arange(2 * w, dtype=I32)[:, None]
    swa_dist = (w + jnp.arange(w, dtype=I32))[None, :] - key
    band = (swa_dist >= 0) & (swa_dist < SWA_WINDOW)
    first = jnp.stack([band & (key >= w), band])
    swa_bkt = jnp.where(first, _rel_bucket(swa_dist)[None], REL_BUCKETS)
    n_tab = min(nblk, FAR_TILE + 1)
    pos = jnp.arange(MOBA_BLOCK, dtype=I32)
    moba_dist = (jnp.arange(n_tab, dtype=I32)[:, None, None] * MOBA_BLOCK
                 + pos[None, None, :] - pos[None, :, None])
    moba_bkt = jnp.where(moba_dist >= 0, _rel_bucket(moba_dist), REL_BUCKETS)
    swa_tab = _bias_table(rel_bias, swa_bkt, SWA_Q_HEADS, 0, scale=LOG2E, group=SWA_GROUP)
    moba_tab = _bias_table(rel_bias, moba_bkt, MOBA_HEADS, SWA_Q_HEADS, scale=LOG2E)

    c_pad = jnp.zeros((8, d), F32).at[:b].set(c)
    mod = _ada_mod(c_pad, w_ada, b_ada)[:, :b]

    w_qkv_bf = w_qkv.astype(BF16)
    w_o_bf = w_o.astype(BF16)
    wg_all = w_gate.reshape(depth * N_EXPERTS, d, D_EXPERT)
    wu_all = w_up.reshape(depth * N_EXPERTS, d, D_EXPERT)
    wd_all = w_down.reshape(depth * N_EXPERTS, D_EXPERT, d)

    pad = jnp.zeros((depth, d, LANES - N_GROUPS - N_EXPERTS), F32)
    wr = jnp.concatenate([w_group, w_expert_router, pad], axis=2)
    wr_hi = wr.astype(BF16)
    wr2_all = jnp.concatenate([wr_hi, (wr - wr_hi.astype(F32)).astype(BF16)], axis=2)
    br_all = jnp.concatenate([b_group, b_expert_router, pad[:, 0, :]], axis=1).reshape(depth, 1, LANES)

    n_tiles = (t * 2) // EXPERT_ROW_TILE + N_EXPERTS
    xg = jnp.zeros((n_tiles * EXPERT_ROW_TILE, d), F32)
    x2 = x.reshape(t, d)
    fg = final_g.reshape(1, d)
    for l in range(depth):
        sh1, sc1, g1, sh2, sc2, g2 = [mod[l, :, k * d:(k + 1) * d].reshape(b, 1, d) for k in range(6)]
        qkv = _qkv_proj(x2, sh1, sc1, norm1_g[l].reshape(1, d), w_qkv_bf[l], s)
        qkv3 = qkv.reshape(b, s, D_QKV)
        ya = _swa_attention(qkv3, attn_sinks[l], swa_tab, swa_out_g[l].reshape(1, D_SWA))
        yb = _moba_attention(qkv3, moba_tab)
        x2, h2, logits = _oproj(ya.reshape(t, D_SWA), yb.reshape(t, D_MOBA), x2,
                                moba_out_g[l].reshape(1, D_MOBA), w_o_bf[l], g1,
                                norm2_g[l].reshape(1, d), sc2, sh2, wr2_all[l], br_all[l], s)
        ri, rf, cnt = _route(logits)
        d0, d1, tile_expert, n_used = _routing_plan(ri, cnt, n_tiles)
        xg = _dispatch(d0, d1, h2, xg)
        y = _expert_ffn(tile_expert, n_used, xg, wg_all, wu_all, wd_all, l)
        x2 = _combine(d0, d1, x2, rf, g2, fg, y, s, final=(l == depth - 1))
    return x2.reshape(b, s, d)
```

```python
import functools
import math

import jax
import jax.numpy as jnp
from jax import lax
from jax.experimental import pallas as pl
from jax.experimental.pallas import tpu as pltpu

F32 = jnp.float32
BF16 = jnp.bfloat16
I32 = jnp.int32
HIGHEST = lax.Precision.HIGHEST

D_MODEL = 2048
HEAD_DIM = 64
SWA_Q_HEADS = 16
SWA_KV_HEADS = 2
SWA_GROUP = SWA_Q_HEADS // SWA_KV_HEADS
SWA_WINDOW = 128
MOBA_HEADS = 16
MOBA_BLOCK = 256
MOBA_TOPK = 3
D_SWA = SWA_Q_HEADS * HEAD_DIM
D_SWA_KV = SWA_KV_HEADS * HEAD_DIM
D_MOBA = MOBA_HEADS * HEAD_DIM
D_QKV = D_SWA + 2 * D_SWA_KV + 3 * D_MOBA
REL_BUCKETS = 32
REL_MAX_DISTANCE = 2048
N_GROUPS = 4
EXPERTS_PER_GROUP = 8
N_EXPERTS = N_GROUPS * EXPERTS_PER_GROUP
D_EXPERT = 512
EPS = 1e-6
NEG = -1e30
SCALE = 1.0 / math.sqrt(HEAD_DIM)
LOG2E = math.log2(math.e)

LANES = 128
BF16_ROWS = 16
V_ROWS = HEAD_DIM + BF16_ROWS
MOBA_STEP_HEADS = 4
FAR_TILE = REL_MAX_DISTANCE // MOBA_BLOCK + 1
QKV_COL_TILE = D_QKV // 2
QKV_ROW_TILE = 1024
OPROJ_ROW_TILE = 512
ROUTE_ROW_TILE = 512
MOVE_ROW_TILE = 256
EXPERT_ROW_TILE = 256
ADA_COL_TILE = 2048
ROUTE_INT_COLS = 8
DMA_UNROLL = 8
VMEM_LIMIT = 56 * 1024 * 1024


def _cparams(sem, vmem=VMEM_LIMIT):
    return pltpu.CompilerParams(dimension_semantics=sem, vmem_limit_bytes=vmem)


def _rel_bucket(dist):
    n = jnp.maximum(dist, 0)
    max_exact = REL_BUCKETS // 2
    nf = jnp.maximum(n, 1).astype(F32)
    large = max_exact + (jnp.log(nf / max_exact) / math.log(REL_MAX_DISTANCE / max_exact)
                         * (REL_BUCKETS - max_exact)).astype(I32)
    large = jnp.minimum(large, REL_BUCKETS - 1)
    return jnp.where(n < max_exact, n, large)


def _silu(v):
    return v * (1.0 / (1.0 + jnp.exp(-v)))


def _ada_kernel(c_ref, w_ref, b_ref, o_ref):
    ca = _silu(c_ref[...])
    o_ref[0] = jnp.dot(ca, w_ref[0], precision=HIGHEST, preferred_element_type=F32) + b_ref[0]


def _ada_mod(c_pad, w_ada, b_ada):
    depth, d, n = w_ada.shape
    rows = c_pad.shape[0]
    return pl.pallas_call(
        _ada_kernel,
        out_shape=jax.ShapeDtypeStruct((depth, rows, n), F32),
        grid=(depth, n // ADA_COL_TILE),
        in_specs=[pl.BlockSpec((rows, d), lambda l, j: (0, 0)),
                  pl.BlockSpec((1, d, ADA_COL_TILE), lambda l, j: (l, 0, j)),
                  pl.BlockSpec((1, 1, ADA_COL_TILE), lambda l, j: (l, 0, j))],
        out_specs=pl.BlockSpec((1, rows, ADA_COL_TILE), lambda l, j: (l, 0, j)),
        compiler_params=_cparams(("arbitrary", "arbitrary")),
        name="ada_mod",
    )(c_pad, w_ada, b_ada.reshape(depth, 1, n))


def _bias_kernel(rb_ref, bkt_ref, o_ref, *, head0, scale):
    h = pl.program_id(0) + head0
    b = bkt_ref[0]
    acc = jnp.full(b.shape, NEG, F32)
    for k in range(REL_BUCKETS):
        acc = jnp.where(b == k, rb_ref[k, h] * scale, acc)
    o_ref[0, 0] = acc


def _bias_table(rel_bias, bkt, n_heads, head0, scale=1.0, group=1):
    nt, r, c = bkt.shape
    return pl.pallas_call(
        functools.partial(_bias_kernel, head0=head0, scale=scale),
        out_shape=jax.ShapeDtypeStruct((n_heads // group, nt, r, c * group), F32),
        grid=(n_heads, nt),
        in_specs=[pl.BlockSpec(memory_space=pltpu.SMEM),
                  pl.BlockSpec((1, r, c), lambda h, t: (t, 0, 0))],
        out_specs=pl.BlockSpec((1, 1, r, c), lambda h, t: (h // group, t, 0, h % group)),
        compiler_params=_cparams(("arbitrary", "arbitrary")),
        name="bias_table",
    )(rel_bias, bkt)


def _rms(v):
    return lax.rsqrt(jnp.mean(v * v, axis=-1, keepdims=True) + EPS)


def _qkv_kernel(x_ref, sh_ref, sc_ref, g_ref, w_ref, o_ref):
    x = x_ref[...]
    h = (x * _rms(x)) * g_ref[...] * (1.0 + sc_ref[0]) + sh_ref[0]
    o_ref[...] = jnp.dot(h.astype(BF16), w_ref[...], preferred_element_type=F32).astype(BF16)


def _qkv_proj(x2, sh, sc, g, w_bf, seq):
    t, d = x2.shape
    tm = QKV_ROW_TILE
    per_b = seq // tm
    return pl.pallas_call(
        _qkv_kernel,
        out_shape=jax.ShapeDtypeStruct((t, D_QKV), BF16),
        grid=(D_QKV // QKV_COL_TILE, t // tm),
        in_specs=[pl.BlockSpec((tm, d), lambda j, i: (i, 0)),
                  pl.BlockSpec((1, 1, d), lambda j, i: (i // per_b, 0, 0)),
                  pl.BlockSpec((1, 1, d), lambda j, i: (i // per_b, 0, 0)),
                  pl.BlockSpec((1, d), lambda j, i: (0, 0)),
                  pl.BlockSpec((d, QKV_COL_TILE), lambda j, i: (0, j))],
        out_specs=pl.BlockSpec((tm, QKV_COL_TILE), lambda j, i: (i, j)),
        compiler_params=_cparams(("arbitrary", "arbitrary")),
        name="qkv_proj",
    )(x2, sh, sc, g, w_bf)


def _swa_kernel(sink_ref, q_ref, kvc_ref, kvp_ref, bias_ref, g_ref, o_ref):
    n = pl.program_id(1)
    w = SWA_WINDOW
    tsel = jnp.minimum(n, 1)
    qt = (q_ref[0].astype(F32) * (SCALE * LOG2E)).T.astype(BF16)
    kv = jnp.concatenate([kvp_ref[0], kvc_ref[0]], axis=0)
    k_all = kv[:, :D_SWA_KV]
    vt = kv[:, D_SWA_KV:].astype(F32).T.astype(BF16)
    zeros = jnp.zeros((HEAD_DIM, SWA_GROUP * w), BF16)
    scores = []
    for kh in range(SWA_KV_HEADS):
        h0 = kh * SWA_GROUP
        q_kh = jnp.concatenate([qt[(h0 + g) * HEAD_DIM:(h0 + g + 1) * HEAD_DIM, :]
                                for g in range(SWA_GROUP)], axis=1)
        rhs = jnp.concatenate([q_kh if j == kh else zeros for j in range(SWA_KV_HEADS)], axis=0)
        scores.append(jnp.dot(k_all, rhs, preferred_element_type=F32) + bias_ref[kh, tsel])
    outs = []
    for kh in range(SWA_KV_HEADS):
        h0 = kh * SWA_GROUP
        st = scores[kh]
        sink = jnp.concatenate([jnp.full((1, w), sink_ref[h0 + g] * LOG2E, F32)
                                for g in range(SWA_GROUP)], axis=1)
        m = jnp.maximum(jnp.max(st, axis=0, keepdims=True), sink)
        e = jnp.exp2(st - m)
        denom = jnp.sum(e, axis=0, keepdims=True) + jnp.exp2(sink - m)
        pv = jnp.dot(vt[kh * HEAD_DIM:(kh + 1) * HEAD_DIM, :], e.astype(BF16),
                     preferred_element_type=F32)
        o = pv * (1.0 / denom)
        outs.extend(o[:, g * w:(g + 1) * w] for g in range(SWA_GROUP))
    y = jnp.concatenate(outs, axis=0).T
    o_ref[0] = ((y * _rms(y)) * g_ref[...]).astype(BF16)


def _swa_attention(qkv3, sinks, bias_tab, out_g):
    b, s, _ = qkv3.shape
    w = SWA_WINDOW
    kv_blk = D_SWA // (2 * D_SWA_KV)
    return pl.pallas_call(
        _swa_kernel,
        out_shape=jax.ShapeDtypeStruct((b, s, D_SWA), BF16),
        grid=(b, s // w),
        in_specs=[pl.BlockSpec(memory_space=pltpu.SMEM),
                  pl.BlockSpec((1, w, D_SWA), lambda bi, n: (bi, n, 0)),
                  pl.BlockSpec((1, w, 2 * D_SWA_KV), lambda bi, n: (bi, n, kv_blk)),
                  pl.BlockSpec((1, w, 2 * D_SWA_KV), lambda bi, n: (bi, jnp.maximum(n - 1, 0), kv_blk)),
                  pl.BlockSpec((SWA_KV_HEADS, 2, 2 * w, SWA_GROUP * w), lambda bi, n: (0, 0, 0, 0)),
                  pl.BlockSpec((1, D_SWA), lambda bi, n: (0, 0))],
        out_specs=pl.BlockSpec((1, w, D_SWA), lambda bi, n: (bi, n, 0)),
        compiler_params=_cparams(("arbitrary", "arbitrary")),
        name="swa_attention",
    )(sinks, qkv3, qkv3, qkv3, bias_tab, out_g)


def _moba_kernel(q_ref, k_ref, v_ref, bias_ref, o_ref,
                 vt_sc, kmean_sc, rhs_sc, sel_sc, s_sc, p_sc, mb_sc, al_sc, al2_sc, m_sc, acc_sc, *, nblk):
    c = pl.program_id(2)
    blk = MOBA_BLOCK
    nh = MOBA_STEP_HEADS
    far = bias_ref.shape[1] - 1

    @pl.when(c == 0)
    def _prep():
        def body(j, carry):
            off = pl.multiple_of(j * blk, blk)
            vt = v_ref[0, pl.ds(off, blk), :].astype(F32).T.astype(BF16)
            ones = jnp.ones((BF16_ROWS, blk), BF16)
            parts = []
            for hh in range(nh):
                parts += [vt[hh * HEAD_DIM:(hh + 1) * HEAD_DIM], ones]
            vt_sc[j] = jnp.concatenate(parts, axis=0)
            kb = k_ref[0, pl.ds(off, blk), :].astype(F32)
            kmean_sc[pl.ds(j, 1), :] = jnp.sum(kb, axis=0, keepdims=True) * (1.0 / blk)
            return carry
        lax.fori_loop(0, nblk, body, 0)

    qt = q_ref[0].astype(F32).T
    row = lax.broadcasted_iota(I32, (nh * HEAD_DIM, blk), 0)
    blk_i = lax.broadcasted_iota(I32, (nblk, blk), 0)
    for hh in range(nh):
        in_head = jnp.where(row >= hh * HEAD_DIM, jnp.where(row < (hh + 1) * HEAD_DIM, 1, 0), 0) > 0
        qt_h = jnp.where(in_head, qt, 0.0)
        rhs_sc[hh] = (qt_h * (SCALE * LOG2E)).astype(BF16)
        gate = jnp.dot(kmean_sc[...], qt_h, precision=HIGHEST, preferred_element_type=F32)
        gate = jnp.where(blk_i < c, gate, NEG)
        sel = jnp.where(blk_i == c, 1.0, 0.0)
        for _ in range(min(MOBA_TOPK, nblk)):
            mx = jnp.max(gate, axis=0, keepdims=True)
            idx = jnp.min(jnp.where(gate == mx, blk_i, nblk), axis=0, keepdims=True)
            hit = blk_i == idx
            sel = jnp.where(hit, jnp.where(blk_i < c, 1.0, sel), sel)
            gate = jnp.where(hit, -jnp.inf, gate)
        sel_sc[hh] = sel
        m_sc[hh] = jnp.full((1, blk), NEG, F32)
        acc_sc[hh] = jnp.zeros((V_ROWS, blk), F32)

    def stage_a(u):
        m = [m_sc[hh] for hh in range(nh)]
        for ti in range(2):
            jt = 2 * u + ti
            off = pl.multiple_of(jt * blk, blk)
            kj = k_ref[0, pl.ds(off, blk), :]
            d = jnp.clip(c - jt, 0, far)
            for hh in range(nh):
                st = jnp.dot(kj, rhs_sc[hh], preferred_element_type=F32) + bias_ref[hh, d]
                s_sc[ti, hh] = st
                chosen = sel_sc[hh, pl.ds(jt, 1), :] > 0.5
                m_big = jnp.maximum(m[hh], jnp.max(st, axis=0, keepdims=True))
                m_new = jnp.where(chosen, m_big, m[hh])
                mb_sc[ti, hh] = m_big
                al_sc[ti, hh] = jnp.exp2(m[hh] - m_new)
                m[hh] = m_new
        for hh in range(nh):
            m_sc[hh] = m[hh]

    def stage_b_exp():
        for ti in range(2):
            for hh in range(nh):
                p_sc[ti, hh] = jnp.exp2(s_sc[ti, hh] - mb_sc[ti, hh]).astype(BF16)

    def stage_b_pv(u):
        for ti in range(2):
            jt = 2 * u + ti
            for hh in range(nh):
                chosen = sel_sc[hh, pl.ds(jt, 1), :] > 0.5
                pv = jnp.dot(vt_sc[jt, pl.ds(hh * V_ROWS, V_ROWS), :], p_sc[ti, hh],
                             preferred_element_type=F32)
                acc_sc[hh] = al2_sc[ti, hh] * acc_sc[hh] + jnp.where(chosen, pv, 0.0)

    n_pairs = (c + 2) // 2
    stage_a(0)

    def body(u, carry):
        stage_b_exp()
        al2_sc[...] = al_sc[...]
        stage_a(u)
        stage_b_pv(u - 1)
        return carry
    lax.fori_loop(1, n_pairs, body, 0)
    stage_b_exp()
    al2_sc[...] = al_sc[...]
    stage_b_pv(n_pairs - 1)

    outs = []
    for hh in range(nh):
        inv_l = 1.0 / acc_sc[hh, HEAD_DIM:HEAD_DIM + 1, :]
        outs.append(acc_sc[hh, :HEAD_DIM, :] * inv_l)
    o_ref[0] = jnp.concatenate(outs, axis=0).T


def _moba_attention(qkv3, bias_tab):
    b, s, _ = qkv3.shape
    blk = MOBA_BLOCK
    nblk = s // blk
    assert nblk % 2 == 0
    nh = MOBA_STEP_HEADS
    groups = MOBA_HEADS // nh
    pw = nh * HEAD_DIM
    q0 = (D_SWA + 2 * D_SWA_KV) // pw
    k0 = q0 + D_MOBA // pw
    v0 = k0 + D_MOBA // pw
    nt = bias_tab.shape[1]
    return pl.pallas_call(
        functools.partial(_moba_kernel, nblk=nblk),
        out_shape=jax.ShapeDtypeStruct((b, s, D_MOBA), F32),
        grid=(b, groups, nblk),
        in_specs=[pl.BlockSpec((1, blk, pw), lambda bi, hp, c: (bi, c, q0 + hp)),
                  pl.BlockSpec((1, s, pw), lambda bi, hp, c: (bi, 0, k0 + hp)),
                  pl.BlockSpec((1, s, pw), lambda bi, hp, c: (bi, 0, v0 + hp)),
                  pl.BlockSpec((nh, nt, blk, blk), lambda bi, hp, c: (hp, 0, 0, 0))],
        out_specs=pl.BlockSpec((1, blk, pw), lambda bi, hp, c: (bi, c, hp)),
        scratch_shapes=[pltpu.VMEM((nblk, nh * V_ROWS, blk), BF16),
                        pltpu.VMEM((nblk, pw), F32),
                        pltpu.VMEM((nh, pw, blk), BF16),
                        pltpu.VMEM((nh, nblk, blk), F32),
                        pltpu.VMEM((2, nh, blk, blk), F32),
                        pltpu.VMEM((2, nh, blk, blk), BF16),
                        pltpu.VMEM((2, nh, 1, blk), F32),
                        pltpu.VMEM((2, nh, 1, blk), F32),
                        pltpu.VMEM((2, nh, 1, blk), F32),
                        pltpu.VMEM((nh, 1, blk), F32),
                        pltpu.VMEM((nh, V_ROWS, blk), F32)],
        compiler_params=_cparams(("arbitrary", "arbitrary", "arbitrary")),
        name="moba_attention",
    )(qkv3, qkv3, qkv3, bias_tab)


def _oproj_kernel(ya_ref, yb_ref, x_ref, gb_ref, wo_ref, g1_ref, n2g_ref, sc2_ref, sh2_ref,
                  wr2_ref, br_ref, xo_ref, h2_ref, lg_ref):
    yb = yb_ref[...]
    ybn = ((yb * _rms(yb)) * gb_ref[...]).astype(BF16)
    y = (jnp.dot(ya_ref[...], wo_ref[0:D_SWA, :], preferred_element_type=F32)
         + jnp.dot(ybn, wo_ref[D_SWA:D_SWA + D_MOBA, :], preferred_element_type=F32))
    xn = x_ref[...] + g1_ref[0] * y
    xo_ref[...] = xn
    h2 = (xn * _rms(xn)) * n2g_ref[...] * (1.0 + sc2_ref[0]) + sh2_ref[0]
    h2_ref[...] = h2
    hi = h2.astype(BF16)
    lo = (h2 - hi.astype(F32)).astype(BF16)
    both = jnp.dot(hi, wr2_ref[...], preferred_element_type=F32)
    lg_ref[...] = (both[:, :LANES] + both[:, LANES:]
                   + jnp.dot(lo, wr2_ref[:, :LANES], preferred_element_type=F32) + br_ref[...])


def _oproj(ya, yb, x2, gb, wo_bf, g1, n2g, sc2, sh2, wr2, br, seq):
    t, d = x2.shape
    tm = OPROJ_ROW_TILE
    per_b = seq // tm
    row = lambda i: (i, 0)
    fix = lambda i: (0, 0)
    bat = lambda i: (i // per_b, 0, 0)
    return pl.pallas_call(
        _oproj_kernel,
        out_shape=(jax.ShapeDtypeStruct((t, d), F32), jax.ShapeDtypeStruct((t, d), F32),
                   jax.ShapeDtypeStruct((t, LANES), F32)),
        grid=(t // tm,),
        in_specs=[pl.BlockSpec((tm, D_SWA), row), pl.BlockSpec((tm, D_MOBA), row),
                  pl.BlockSpec((tm, d), row), pl.BlockSpec((1, D_MOBA), fix),
                  pl.BlockSpec((D_SWA + D_MOBA, d), fix), pl.BlockSpec((1, 1, d), bat),
                  pl.BlockSpec((1, d), fix), pl.BlockSpec((1, 1, d), bat), pl.BlockSpec((1, 1, d), bat),
                  pl.BlockSpec((d, 2 * LANES), fix), pl.BlockSpec((1, LANES), fix)],
        out_specs=(pl.BlockSpec((tm, d), row), pl.BlockSpec((tm, d), row), pl.BlockSpec((tm, LANES), row)),
        compiler_params=_cparams(("arbitrary",)),
        name="oproj",
    )(ya, yb, x2, gb, wo_bf, g1, n2g, sc2, sh2, wr2, br)


def _route_kernel(lg_ref, ri_ref, rf_ref, cnt_ref, carry_sc):
    i = pl.program_id(0)

    @pl.when(i == 0)
    def _init():
        carry_sc[...] = jnp.zeros_like(carry_sc)

    lg = lg_ref[...]
    tm = lg.shape[0]
    lane = lax.broadcasted_iota(I32, (tm, LANES), 1)
    ninf = -jnp.inf
    gl = jnp.where(lane < N_GROUPS, lg, ninf)
    ge = jnp.exp(gl - jnp.max(gl, axis=-1, keepdims=True))
    gp = ge / jnp.sum(ge, axis=-1, keepdims=True)
    p_grp = jnp.max(gp, axis=-1, keepdims=True)
    g_idx = jnp.min(jnp.where(gp == p_grp, lane, LANES), axis=-1, keepdims=True)
    lo = N_GROUPS + g_idx * EXPERTS_PER_GROUP
    in_grp = jnp.where(lane >= lo, jnp.where(lane < lo + EXPERTS_PER_GROUP, 1, 0), 0) > 0
    el = jnp.where(in_grp, lg, ninf)
    v0 = jnp.max(el, axis=-1, keepdims=True)
    i0 = jnp.min(jnp.where(el == v0, lane, LANES), axis=-1, keepdims=True)
    el = jnp.where(lane == i0, ninf, el)
    v1 = jnp.max(el, axis=-1, keepdims=True)
    i1 = jnp.min(jnp.where(el == v1, lane, LANES), axis=-1, keepdims=True)
    e0 = i0 - N_GROUPS
    e1 = i1 - N_GROUPS
    ex1 = jnp.exp(v1 - v0)
    den = 1.0 + ex1
    w0 = p_grp * (1.0 / den)
    w1 = p_grp * (ex1 / den)
    onehot = jnp.where(lane == e0, 1.0, jnp.where(lane == e1, 1.0, 0.0))
    r_i = lax.broadcasted_iota(I32, (tm, tm), 0)
    c_i = lax.broadcasted_iota(I32, (tm, tm), 1)
    tri = jnp.where(r_i > c_i, 1.0, 0.0).astype(BF16)
    cum = jnp.dot(tri, onehot.astype(BF16), preferred_element_type=F32) + carry_sc[...]
    pos0 = jnp.sum(jnp.where(lane == e0, cum, 0.0), axis=-1, keepdims=True).astype(I32)
    pos1 = jnp.sum(jnp.where(lane == e1, cum, 0.0), axis=-1, keepdims=True).astype(I32)
    carry_sc[...] = carry_sc[...] + jnp.sum(onehot, axis=0, keepdims=True)
    cnt_ref[...] = carry_sc[...]
    packed = jnp.where(lane == 0, e0, jnp.where(lane == 1, e1,
                       jnp.where(lane == 2, pos0, jnp.where(lane == 3, pos1, 0))))
    ri_ref[...] = packed[:, :ROUTE_INT_COLS]
    rf_ref[...] = jnp.where(lane == 0, w0, jnp.where(lane == 1, w1, 0.0))


def _route(logits):
    t = logits.shape[0]
    tm = ROUTE_ROW_TILE
    return pl.pallas_call(
        _route_kernel,
        out_shape=(jax.ShapeDtypeStruct((t, ROUTE_INT_COLS), I32), jax.ShapeDtypeStruct((t, LANES), F32),
                   jax.ShapeDtypeStruct((1, LANES), F32)),
        grid=(t // tm,),
        in_specs=[pl.BlockSpec((tm, LANES), lambda i: (i, 0))],
        out_specs=(pl.BlockSpec((tm, ROUTE_INT_COLS), lambda i: (i, 0)), pl.BlockSpec((tm, LANES), lambda i: (i, 0)),
                   pl.BlockSpec((1, LANES), lambda i: (0, 0))),
        scratch_shapes=[pltpu.VMEM((1, LANES), F32)],
        compiler_params=_cparams(("arbitrary",)),
        name="route",
    )(logits)


def _row_copy(src, dst, sem):
    return pltpu.make_async_copy(src, dst, sem)


def _dispatch_kernel(d0_ref, d1_ref, h_ref, xin_ref, xg_ref, sem):
    del xin_ref
    tm = h_ref.shape[0]
    base = pl.program_id(0) * tm

    def issue(r, carry):
        src = h_ref.at[pl.ds(r, 1), :]
        _row_copy(src, xg_ref.at[pl.ds(d0_ref[base + r], 1), :], sem.at[0]).start(priority=0)
        _row_copy(src, xg_ref.at[pl.ds(d1_ref[base + r], 1), :], sem.at[0]).start(priority=1)
        return carry
    lax.fori_loop(0, tm, issue, 0, unroll=DMA_UNROLL)

    def drain(r, carry):
        src = h_ref.at[pl.ds(r, 1), :]
        _row_copy(src, xg_ref.at[pl.ds(d0_ref[base + r], 1), :], sem.at[0]).wait()
        _row_copy(src, xg_ref.at[pl.ds(d1_ref[base + r], 1), :], sem.at[0]).wait()
        return carry
    lax.fori_loop(0, tm, drain, 0, unroll=DMA_UNROLL)


def _dispatch(d0, d1, h2, xg):
    t, d = h2.shape
    tm = MOVE_ROW_TILE
    return pl.pallas_call(
        _dispatch_kernel,
        out_shape=jax.ShapeDtypeStruct(xg.shape, xg.dtype),
        grid_spec=pltpu.PrefetchScalarGridSpec(
            num_scalar_prefetch=2, grid=(t // tm,),
            in_specs=[pl.BlockSpec((tm, d), lambda i, a, b: (i, 0)),
                      pl.BlockSpec(memory_space=pl.ANY)],
            out_specs=pl.BlockSpec(memory_space=pl.ANY),
            scratch_shapes=[pltpu.SemaphoreType.DMA((1,))]),
        input_output_aliases={3: 0},
        compiler_params=_cparams(("arbitrary",)),
        name="dispatch",
    )(d0, d1, h2, xg)


def _ffn_kernel(te_ref, nu_ref, x_ref, wg_ref, wu_ref, wd_ref, y_ref, wg_sc, wu_sc, wd_sc):
    i = pl.program_id(0)
    new_expert = jnp.logical_or(i == 0, te_ref[i] != te_ref[jnp.maximum(i - 1, 0)])

    @pl.when(new_expert)
    def _cast():
        wg_sc[...] = wg_ref[0].astype(BF16)
        wu_sc[...] = wu_ref[0].astype(BF16)
        wd_sc[...] = wd_ref[0].astype(BF16)

    @pl.when(i < nu_ref[0])
    def _compute():
        xb = x_ref[...].astype(BF16)
        g = jnp.dot(xb, wg_sc[...], preferred_element_type=F32)
        u = jnp.dot(xb, wu_sc[...], preferred_element_type=F32)
        hid = (_silu(g) * u).astype(BF16)
        y_ref[...] = jnp.dot(hid, wd_sc[...], preferred_element_type=F32)

    @pl.when(i >= nu_ref[0])
    def _skip():
        y_ref[...] = jnp.zeros_like(y_ref)


def _expert_ffn(tile_expert, n_used, xg, wg, wu, wd, layer):
    npad, d = xg.shape
    tr = EXPERT_ROW_TILE
    e_base = layer * N_EXPERTS
    wmap = lambda i, te, nu: (e_base + te[i], 0, 0)
    return pl.pallas_call(
        _ffn_kernel,
        out_shape=jax.ShapeDtypeStruct((npad, d), F32),
        grid_spec=pltpu.PrefetchScalarGridSpec(
            num_scalar_prefetch=2, grid=(npad // tr,),
            in_specs=[pl.BlockSpec((tr, d), lambda i, te, nu: (jnp.minimum(i, nu[0] - 1), 0)),
                      pl.BlockSpec((1, d, D_EXPERT), wmap),
                      pl.BlockSpec((1, d, D_EXPERT), wmap),
                      pl.BlockSpec((1, D_EXPERT, d), wmap)],
            out_specs=pl.BlockSpec((tr, d), lambda i, te, nu: (i, 0)),
            scratch_shapes=[pltpu.VMEM((d, D_EXPERT), BF16), pltpu.VMEM((d, D_EXPERT), BF16),
                            pltpu.VMEM((D_EXPERT, d), BF16)]),
        compiler_params=_cparams(("arbitrary",)),
        name="expert_ffn",
    )(tile_expert, n_used, xg, wg, wu, wd)


def _combine_kernel(d0_ref, d1_ref, x_ref, rf_ref, g2_ref, fg_ref, y_ref, o_ref, buf, sem, *, final):
    tm = x_ref.shape[0]
    base = pl.program_id(0) * tm

    def issue(r, carry):
        _row_copy(y_ref.at[pl.ds(d0_ref[base + r], 1), :], buf.at[0, pl.ds(r, 1), :], sem.at[0]).start(priority=0)
        _row_copy(y_ref.at[pl.ds(d1_ref[base + r], 1), :], buf.at[1, pl.ds(r, 1), :], sem.at[0]).start(priority=1)
        return carry
    lax.fori_loop(0, tm, issue, 0, unroll=DMA_UNROLL)

    def drain(r, carry):
        _row_copy(y_ref.at[pl.ds(d0_ref[base + r], 1), :], buf.at[0, pl.ds(r, 1), :], sem.at[0]).wait()
        _row_copy(y_ref.at[pl.ds(d1_ref[base + r], 1), :], buf.at[1, pl.ds(r, 1), :], sem.at[0]).wait()
        return carry
    lax.fori_loop(0, tm, drain, 0, unroll=DMA_UNROLL)

    w0 = rf_ref[:, 0:1]
    w1 = rf_ref[:, 1:2]
    xn = x_ref[...] + g2_ref[0] * (w0 * buf[0] + w1 * buf[1])
    if final:
        xn = (xn * _rms(xn)) * fg_ref[...]
    o_ref[...] = xn


def _combine(d0, d1, x2, rf, g2, final_g, y, seq, final):
    t, d = x2.shape
    tm = MOVE_ROW_TILE
    per_b = seq // tm
    return pl.pallas_call(
        functools.partial(_combine_kernel, final=final),
        out_shape=jax.ShapeDtypeStruct((t, d), F32),
        grid_spec=pltpu.PrefetchScalarGridSpec(
            num_scalar_prefetch=2, grid=(t // tm,),
            in_specs=[pl.BlockSpec((tm, d), lambda i, a, b: (i, 0)),
                      pl.BlockSpec((tm, LANES), lambda i, a, b: (i, 0)),
                      pl.BlockSpec((1, 1, d), lambda i, a, b: (i // per_b, 0, 0)),
                      pl.BlockSpec((1, d), lambda i, a, b: (0, 0)),
                      pl.BlockSpec(memory_space=pl.ANY)],
            out_specs=pl.BlockSpec((tm, d), lambda i, a, b: (i, 0)),
            scratch_shapes=[pltpu.VMEM((2, tm, d), F32), pltpu.SemaphoreType.DMA((1,))]),
        compiler_params=_cparams(("arbitrary",)),
        name="combine",
    )(d0, d1, x2, rf, g2, final_g, y)


def _routing_plan(ri, cnt, n_tiles):
    tr = EXPERT_ROW_TILE
    counts = cnt[0, :N_EXPERTS].astype(I32)
    tiles_e = (counts + tr - 1) // tr
    tile_end = jnp.cumsum(tiles_e)
    row_off = (tile_end - tiles_e) * tr
    d0 = jnp.take(row_off, ri[:, 0]) + ri[:, 2]
    d1 = jnp.take(row_off, ri[:, 1]) + ri[:, 3]
    n_used = tile_end[-1]
    tid = jnp.minimum(jnp.arange(n_tiles, dtype=I32), n_used - 1)
    tile_expert = jnp.sum((tid[:, None] >= tile_end[None, :]).astype(I32), axis=1)
    return d0, d1, tile_expert.astype(I32), n_used.reshape(1).astype(I32)


def kernel(x, c, rel_bias, w_ada, b_ada, norm1_g, norm2_g, w_qkv, attn_sinks, swa_out_g, moba_out_g,
           w_o, w_group, b_group, w_expert_router, b_expert_router, w_gate, w_up, w_down, final_g):
    b, s, d = x.shape
    depth = w_ada.shape[0]
    t = b * s
    nblk = s // MOBA_BLOCK

    w = SWA_WINDOW
    key = jnp.arange(2 * w, dtype=I32)[:, None]
    swa_dist = (w + jnp.arange(w, dtype=I32))[None, :] - key
    band = (swa_dist >= 0) & (swa_dist < SWA_WINDOW)
    first = jnp.stack([band & (key >= w), band])
    swa_bkt = jnp.where(first, _rel_bucket(swa_dist)[None], REL_BUCKETS)
    n_tab = min(nblk, FAR_TILE + 1)
    pos = jnp.arange(MOBA_BLOCK, dtype=I32)
    moba_dist = (jnp.arange(n_tab, dtype=I32)[:, None, None] * MOBA_BLOCK
                 + pos[None, None, :] - pos[None, :, None])
    moba_bkt = jnp.where(moba_dist >= 0, _rel_bucket(moba_dist), REL_BUCKETS)
    swa_tab = _bias_table(rel_bias, swa_bkt, SWA_Q_HEADS, 0, scale=LOG2E, group=SWA_GROUP)
    moba_tab = _bias_table(rel_bias, moba_bkt, MOBA_HEADS, SWA_Q_HEADS, scale=LOG2E)

    c_pad = jnp.zeros((8, d), F32).at[:b].set(c)
    mod = _ada_mod(c_pad, w_ada, b_ada)[:, :b]

    w_qkv_bf = w_qkv.astype(BF16)
    w_o_bf = w_o.astype(BF16)
    wg_all = w_gate.reshape(depth * N_EXPERTS, d, D_EXPERT)
    wu_all = w_up.reshape(depth * N_EXPERTS, d, D_EXPERT)
    wd_all = w_down.reshape(depth * N_EXPERTS, D_EXPERT, d)

    pad = jnp.zeros((depth, d, LANES - N_GROUPS - N_EXPERTS), F32)
    wr = jnp.concatenate([w_group, w_expert_router, pad], axis=2)
    wr_hi = wr.astype(BF16)
    wr2_all = jnp.concatenate([wr_hi, (wr - wr_hi.astype(F32)).astype(BF16)], axis=2)
    br_all = jnp.concatenate([b_group, b_expert_router, pad[:, 0, :]], axis=1).reshape(depth, 1, LANES)

    n_tiles = (t * 2) // EXPERT_ROW_TILE + N_EXPERTS
    xg = jnp.zeros((n_tiles * EXPERT_ROW_TILE, d), F32)
    x2 = x.reshape(t, d)
    fg = final_g.reshape(1, d)
    for l in range(depth):
        sh1, sc1, g1, sh2, sc2, g2 = [mod[l, :, k * d:(k + 1) * d].reshape(b, 1, d) for k in range(6)]
        qkv = _qkv_proj(x2, sh1, sc1, norm1_g[l].reshape(1, d), w_qkv_bf[l], s)
        qkv3 = qkv.reshape(b, s, D_QKV)
        ya = _swa_attention(qkv3, attn_sinks[l], swa_tab, swa_out_g[l].reshape(1, D_SWA))
        yb = _moba_attention(qkv3, moba_tab)
        x2, h2, logits = _oproj(ya.reshape(t, D_SWA), yb.reshape(t, D_MOBA), x2,
                                moba_out_g[l].reshape(1, D_MOBA), w_o_bf[l], g1,
                                norm2_g[l].reshape(1, d), sc2, sh2, wr2_all[l], br_all[l], s)
        ri, rf, cnt = _route(logits)
        d0, d1, tile_expert, n_used = _routing_plan(ri, cnt, n_tiles)
        xg = _dispatch(d0, d1, h2, xg)
        y = _expert_ffn(tile_expert, n_used, xg, wg_all, wu_all, wd_all, l)
        x2 = _combine(d0, d1, x2, rf, g2, fg, y, s, final=(l == depth - 1))
    return x2.reshape(b, s, d)
```

```python
import functools
import math

import jax
import jax.numpy as jnp
from jax import lax
from jax.experimental import pallas as pl
from jax.experimental.pallas import tpu as pltpu

F32 = jnp.float32
BF16 = jnp.bfloat16
I32 = jnp.int32
HIGHEST = lax.Precision.HIGHEST

D_MODEL = 2048
HEAD_DIM = 64
SWA_Q_HEADS = 16
SWA_KV_HEADS = 2
SWA_GROUP = SWA_Q_HEADS // SWA_KV_HEADS
SWA_WINDOW = 128
MOBA_HEADS = 16
MOBA_BLOCK = 256
MOBA_TOPK = 3
D_SWA = SWA_Q_HEADS * HEAD_DIM
D_SWA_KV = SWA_KV_HEADS * HEAD_DIM
D_MOBA = MOBA_HEADS * HEAD_DIM
D_QKV = D_SWA + 2 * D_SWA_KV + 3 * D_MOBA
REL_BUCKETS = 32
REL_MAX_DISTANCE = 2048
N_GROUPS = 4
EXPERTS_PER_GROUP = 8
N_EXPERTS = N_GROUPS * EXPERTS_PER_GROUP
D_EXPERT = 512
EPS = 1e-6
NEG = -1e30
SCALE = 1.0 / math.sqrt(HEAD_DIM)
LOG2E = math.log2(math.e)

LANES = 128
BF16_ROWS = 16
V_ROWS = HEAD_DIM + BF16_ROWS
MOBA_STEP_HEADS = 4
FAR_TILE = REL_MAX_DISTANCE // MOBA_BLOCK + 1
QKV_COL_TILE = D_QKV // 2
QKV_ROW_TILE = 1024
OPROJ_ROW_TILE = 512
ROUTE_ROW_TILE = 512
MOVE_ROW_TILE = 512
EXPERT_ROW_TILE = 256
ADA_COL_TILE = 1024
DMA_UNROLL = 8
VMEM_LIMIT = 56 * 1024 * 1024


def _cparams(sem, vmem=VMEM_LIMIT):
    return pltpu.CompilerParams(dimension_semantics=sem, vmem_limit_bytes=vmem)


def _rel_bucket(dist):
    n = jnp.maximum(dist, 0)
    max_exact = REL_BUCKETS // 2
    nf = jnp.maximum(n, 1).astype(F32)
    large = max_exact + (jnp.log(nf / max_exact) / math.log(REL_MAX_DISTANCE / max_exact)
                         * (REL_BUCKETS - max_exact)).astype(I32)
    large = jnp.minimum(large, REL_BUCKETS - 1)
    return jnp.where(n < max_exact, n, large)


def _silu(v):
    return v * (1.0 / (1.0 + jnp.exp(-v)))


def _ada_kernel(c_ref, w_ref, b_ref, o_ref):
    ca = _silu(c_ref[...])
    o_ref[0] = jnp.dot(ca, w_ref[0], precision=HIGHEST, preferred_element_type=F32) + b_ref[0]


def _ada_mod(c_pad, w_ada, b_ada):
    depth, d, n = w_ada.shape
    rows = c_pad.shape[0]
    return pl.pallas_call(
        _ada_kernel,
        out_shape=jax.ShapeDtypeStruct((depth, rows, n), F32),
        grid=(depth, n // ADA_COL_TILE),
        in_specs=[pl.BlockSpec((rows, d), lambda l, j: (0, 0)),
                  pl.BlockSpec((1, d, ADA_COL_TILE), lambda l, j: (l, 0, j)),
                  pl.BlockSpec((1, 1, ADA_COL_TILE), lambda l, j: (l, 0, j))],
        out_specs=pl.BlockSpec((1, rows, ADA_COL_TILE), lambda l, j: (l, 0, j)),
        compiler_params=_cparams(("arbitrary", "arbitrary")),
        name="ada_mod",
    )(c_pad, w_ada, b_ada.reshape(depth, 1, n))


def _bias_kernel(rb_ref, bkt_ref, o_ref, *, head0, scale):
    h = pl.program_id(0) + head0
    b = bkt_ref[0]
    acc = jnp.full(b.shape, NEG, F32)
    for k in range(REL_BUCKETS):
        acc = jnp.where(b == k, rb_ref[k, h] * scale, acc)
    o_ref[0, 0] = acc


def _bias_table(rel_bias, bkt, n_heads, head0, scale=1.0, group=1):
    nt, r, c = bkt.shape
    return pl.pallas_call(
        functools.partial(_bias_kernel, head0=head0, scale=scale),
        out_shape=jax.ShapeDtypeStruct((n_heads // group, nt, r, c * group), F32),
        grid=(n_heads, nt),
        in_specs=[pl.BlockSpec(memory_space=pltpu.SMEM),
                  pl.BlockSpec((1, r, c), lambda h, t: (t, 0, 0))],
        out_specs=pl.BlockSpec((1, 1, r, c), lambda h, t: (h // group, t, 0, h % group)),
        compiler_params=_cparams(("arbitrary", "arbitrary")),
        name="bias_table",
    )(rel_bias, bkt)


def _rms(v):
    return lax.rsqrt(jnp.mean(v * v, axis=-1, keepdims=True) + EPS)


def _qkv_kernel(x_ref, sh_ref, sc_ref, g_ref, w_ref, o_ref):
    x = x_ref[...]
    h = (x * _rms(x)) * g_ref[...] * (1.0 + sc_ref[0]) + sh_ref[0]
    o_ref[...] = jnp.dot(h.astype(BF16), w_ref[...], preferred_element_type=F32).astype(BF16)


def _qkv_proj(x2, sh, sc, g, w_bf, seq):
    t, d = x2.shape
    tm = QKV_ROW_TILE
    per_b = seq // tm
    return pl.pallas_call(
        _qkv_kernel,
        out_shape=jax.ShapeDtypeStruct((t, D_QKV), BF16),
        grid=(D_QKV // QKV_COL_TILE, t // tm),
        in_specs=[pl.BlockSpec((tm, d), lambda j, i: (i, 0)),
                  pl.BlockSpec((1, 1, d), lambda j, i: (i // per_b, 0, 0)),
                  pl.BlockSpec((1, 1, d), lambda j, i: (i // per_b, 0, 0)),
                  pl.BlockSpec((1, d), lambda j, i: (0, 0)),
                  pl.BlockSpec((d, QKV_COL_TILE), lambda j, i: (0, j))],
        out_specs=pl.BlockSpec((tm, QKV_COL_TILE), lambda j, i: (i, j)),
        compiler_params=_cparams(("arbitrary", "arbitrary")),
        name="qkv_proj",
    )(x2, sh, sc, g, w_bf)


def _swa_kernel(sink_ref, q_ref, kvc_ref, kvp_ref, bias_ref, g_ref, o_ref):
    n = pl.program_id(1)
    w = SWA_WINDOW
    tsel = jnp.minimum(n, 1)
    qt = (q_ref[0].astype(F32) * (SCALE * LOG2E)).T.astype(BF16)
    kv = jnp.concatenate([kvp_ref[0], kvc_ref[0]], axis=0)
    k_all = kv[:, :D_SWA_KV]
    vt = kv[:, D_SWA_KV:].astype(F32).T.astype(BF16)
    zeros = jnp.zeros((HEAD_DIM, SWA_GROUP * w), BF16)
    scores = []
    for kh in range(SWA_KV_HEADS):
        h0 = kh * SWA_GROUP
        q_kh = jnp.concatenate([qt[(h0 + g) * HEAD_DIM:(h0 + g + 1) * HEAD_DIM, :]
                                for g in range(SWA_GROUP)], axis=1)
        rhs = jnp.concatenate([q_kh if j == kh else zeros for j in range(SWA_KV_HEADS)], axis=0)
        scores.append(jnp.dot(k_all, rhs, preferred_element_type=F32) + bias_ref[kh, tsel])
    outs = []
    for kh in range(SWA_KV_HEADS):
        h0 = kh * SWA_GROUP
        st = scores[kh]
        sink = jnp.concatenate([jnp.full((1, w), sink_ref[h0 + g] * LOG2E, F32)
                                for g in range(SWA_GROUP)], axis=1)
        m = jnp.maximum(jnp.max(st, axis=0, keepdims=True), sink)
        e = jnp.exp2(st - m)
        denom = jnp.sum(e, axis=0, keepdims=True) + jnp.exp2(sink - m)
        pv = jnp.dot(vt[kh * HEAD_DIM:(kh + 1) * HEAD_DIM, :], e.astype(BF16),
                     preferred_element_type=F32)
        o = pv * (1.0 / denom)
        outs.extend(o[:, g * w:(g + 1) * w] for g in range(SWA_GROUP))
    y = jnp.concatenate(outs, axis=0).T
    o_ref[0] = ((y * _rms(y)) * g_ref[...]).astype(BF16)


def _swa_attention(qkv3, sinks, bias_tab, out_g):
    b, s, _ = qkv3.shape
    w = SWA_WINDOW
    kv_blk = D_SWA // (2 * D_SWA_KV)
    return pl.pallas_call(
        _swa_kernel,
        out_shape=jax.ShapeDtypeStruct((b, s, D_SWA), BF16),
        grid=(b, s // w),
        in_specs=[pl.BlockSpec(memory_space=pltpu.SMEM),
                  pl.BlockSpec((1, w, D_SWA), lambda bi, n: (bi, n, 0)),
                  pl.BlockSpec((1, w, 2 * D_SWA_KV), lambda bi, n: (bi, n, kv_blk)),
                  pl.BlockSpec((1, w, 2 * D_SWA_KV), lambda bi, n: (bi, jnp.maximum(n - 1, 0), kv_blk)),
                  pl.BlockSpec((SWA_KV_HEADS, 2, 2 * w, SWA_GROUP * w), lambda bi, n: (0, 0, 0, 0)),
                  pl.BlockSpec((1, D_SWA), lambda bi, n: (0, 0))],
        out_specs=pl.BlockSpec((1, w, D_SWA), lambda bi, n: (bi, n, 0)),
        compiler_params=_cparams(("arbitrary", "arbitrary")),
        name="swa_attention",
    )(sinks, qkv3, qkv3, qkv3, bias_tab, out_g)


def _moba_kernel(q_ref, k_ref, v_ref, bias_ref, o_ref,
                 vt_sc, kmean_sc, rhs_sc, sel_sc, s_sc, p_sc, mb_sc, al_sc, al2_sc, m_sc, acc_sc, *, nblk):
    c = pl.program_id(2)
    blk = MOBA_BLOCK
    nh = MOBA_STEP_HEADS
    far = bias_ref.shape[1] - 1

    @pl.when(c == 0)
    def _prep():
        def body(j, carry):
            off = pl.multiple_of(j * blk, blk)
            vt = v_ref[0, pl.ds(off, blk), :].astype(F32).T.astype(BF16)
            ones = jnp.ones((BF16_ROWS, blk), BF16)
            parts = []
            for hh in range(nh):
                parts += [vt[hh * HEAD_DIM:(hh + 1) * HEAD_DIM], ones]
            vt_sc[j] = jnp.concatenate(parts, axis=0)
            kb = k_ref[0, pl.ds(off, blk), :].astype(F32)
            kmean_sc[pl.ds(j, 1), :] = jnp.sum(kb, axis=0, keepdims=True) * (1.0 / blk)
            return carry
        lax.fori_loop(0, nblk, body, 0)

    qt = q_ref[0].astype(F32).T
    row = lax.broadcasted_iota(I32, (nh * HEAD_DIM, blk), 0)
    blk_i = lax.broadcasted_iota(I32, (nblk, blk), 0)
    for hh in range(nh):
        in_head = jnp.where(row >= hh * HEAD_DIM, jnp.where(row < (hh + 1) * HEAD_DIM, 1, 0), 0) > 0
        qt_h = jnp.where(in_head, qt, 0.0)
        rhs_sc[hh] = (qt_h * (SCALE * LOG2E)).astype(BF16)
        gate = jnp.dot(kmean_sc[...], qt_h, precision=HIGHEST, preferred_element_type=F32)
        gate = jnp.where(blk_i < c, gate, NEG)
        sel = jnp.where(blk_i == c, 1.0, 0.0)
        for _ in range(min(MOBA_TOPK, nblk)):
            mx = jnp.max(gate, axis=0, keepdims=True)
            idx = jnp.min(jnp.where(gate == mx, blk_i, nblk), axis=0, keepdims=True)
            hit = blk_i == idx
            sel = jnp.where(hit, jnp.where(blk_i < c, 1.0, sel), sel)
            gate = jnp.where(hit, -jnp.inf, gate)
        sel_sc[hh] = sel
        m_sc[hh] = jnp.full((1, blk), NEG, F32)
        acc_sc[hh] = jnp.zeros((V_ROWS, blk), F32)

    def stage_a(u):
        m = [m_sc[hh] for hh in range(nh)]
        for ti in range(2):
            jt = 2 * u + ti
            off = pl.multiple_of(jt * blk, blk)
            kj = k_ref[0, pl.ds(off, blk), :]
            d = jnp.clip(c - jt, 0, far)
            for hh in range(nh):
                st = jnp.dot(kj, rhs_sc[hh], preferred_element_type=F32) + bias_ref[hh, d]
                s_sc[ti, hh] = st
                chosen = sel_sc[hh, pl.ds(jt, 1), :] > 0.5
                m_big = jnp.maximum(m[hh], jnp.max(st, axis=0, keepdims=True))
                m_new = jnp.where(chosen, m_big, m[hh])
                mb_sc[ti, hh] = m_big
                al_sc[ti, hh] = jnp.exp2(m[hh] - m_new)
                m[hh] = m_new
        for hh in range(nh):
            m_sc[hh] = m[hh]

    def stage_b_exp():
        for ti in range(2):
            for hh in range(nh):
                p_sc[ti, hh] = jnp.exp2(s_sc[ti, hh] - mb_sc[ti, hh]).astype(BF16)

    def stage_b_pv(u):
        for ti in range(2):
            jt = 2 * u + ti
            for hh in range(nh):
                chosen = sel_sc[hh, pl.ds(jt, 1), :] > 0.5
                pv = jnp.dot(vt_sc[jt, pl.ds(hh * V_ROWS, V_ROWS), :], p_sc[ti, hh],
                             preferred_element_type=F32)
                acc_sc[hh] = al2_sc[ti, hh] * acc_sc[hh] + jnp.where(chosen, pv, 0.0)

    n_pairs = (c + 2) // 2
    stage_a(0)

    def body(u, carry):
        stage_b_exp()
        al2_sc[...] = al_sc[...]
        stage_a(u)
        stage_b_pv(u - 1)
        return carry
    lax.fori_loop(1, n_pairs, body, 0)
    stage_b_exp()
    al2_sc[...] = al_sc[...]
    stage_b_pv(n_pairs - 1)

    outs = []
    for hh in range(nh):
        inv_l = 1.0 / acc_sc[hh, HEAD_DIM:HEAD_DIM + 1, :]
        outs.append(acc_sc[hh, :HEAD_DIM, :] * inv_l)
    o_ref[0] = jnp.concatenate(outs, axis=0).T


def _moba_attention(qkv3, bias_tab):
    b, s, _ = qkv3.shape
    blk = MOBA_BLOCK
    nblk = s // blk
    assert nblk % 2 == 0
    nh = MOBA_STEP_HEADS
    groups = MOBA_HEADS // nh
    pw = nh * HEAD_DIM
    q0 = (D_SWA + 2 * D_SWA_KV) // pw
    k0 = q0 + D_MOBA // pw
    v0 = k0 + D_MOBA // pw
    nt = bias_tab.shape[1]
    return pl.pallas_call(
        functools.partial(_moba_kernel, nblk=nblk),
        out_shape=jax.ShapeDtypeStruct((b, s, D_MOBA), F32),
        grid=(b, groups, nblk),
        in_specs=[pl.BlockSpec((1, blk, pw), lambda bi, hp, c: (bi, c, q0 + hp)),
                  pl.BlockSpec((1, s, pw), lambda bi, hp, c: (bi, 0, k0 + hp)),
                  pl.BlockSpec((1, s, pw), lambda bi, hp, c: (bi, 0, v0 + hp)),
                  pl.BlockSpec((nh, nt, blk, blk), lambda bi, hp, c: (hp, 0, 0, 0))],
        out_specs=pl.BlockSpec((1, blk, pw), lambda bi, hp, c: (bi, c, hp)),
        scratch_shapes=[pltpu.VMEM((nblk, nh * V_ROWS, blk), BF16),
                        pltpu.VMEM((nblk, pw), F32),
                        pltpu.VMEM((nh, pw, blk), BF16),
                        pltpu.VMEM((nh, nblk, blk), F32),
                        pltpu.VMEM((2, nh, blk, blk), F32),
                        pltpu.VMEM((2, nh, blk, blk), BF16),
                        pltpu.VMEM((2, nh, 1, blk), F32),
                        pltpu.VMEM((2, nh, 1, blk), F32),
                        pltpu.VMEM((2, nh, 1, blk), F32),
                        pltpu.VMEM((nh, 1, blk), F32),
                        pltpu.VMEM((nh, V_ROWS, blk), F32)],
        compiler_params=_cparams(("arbitrary", "arbitrary", "arbitrary")),
        name="moba_attention",
    )(qkv3, qkv3, qkv3, bias_tab)


def _oproj_kernel(ya_ref, yb_ref, x_ref, gb_ref, wo_ref, g1_ref, n2g_ref, sc2_ref, sh2_ref,
                  wr2_ref, br_ref, xo_ref, h2_ref, lg_ref):
    yb = yb_ref[...]
    ybn = ((yb * _rms(yb)) * gb_ref[...]).astype(BF16)
    y = (jnp.dot(ya_ref[...], wo_ref[0:D_SWA, :], preferred_element_type=F32)
         + jnp.dot(ybn, wo_ref[D_SWA:D_SWA + D_MOBA, :], preferred_element_type=F32))
    xn = x_ref[...] + g1_ref[0] * y
    xo_ref[...] = xn
    h2 = (xn * _rms(xn)) * n2g_ref[...] * (1.0 + sc2_ref[0]) + sh2_ref[0]
    h2_ref[...] = h2
    hi = h2.astype(BF16)
    lo = (h2 - hi.astype(F32)).astype(BF16)
    both = jnp.dot(hi, wr2_ref[...], preferred_element_type=F32)
    lg_ref[...] = (both[:, :LANES] + both[:, LANES:]
                   + jnp.dot(lo, wr2_ref[:, :LANES], preferred_element_type=F32) + br_ref[...])


def _oproj(ya, yb, x2, gb, wo_bf, g1, n2g, sc2, sh2, wr2, br, seq):
    t, d = x2.shape
    tm = OPROJ_ROW_TILE
    per_b = seq // tm
    row = lambda i: (i, 0)
    fix = lambda i: (0, 0)
    bat = lambda i: (i // per_b, 0, 0)
    return pl.pallas_call(
        _oproj_kernel,
        out_shape=(jax.ShapeDtypeStruct((t, d), F32), jax.ShapeDtypeStruct((t, d), F32),
                   jax.ShapeDtypeStruct((t, LANES), F32)),
        grid=(t // tm,),
        in_specs=[pl.BlockSpec((tm, D_SWA), row), pl.BlockSpec((tm, D_MOBA), row),
                  pl.BlockSpec((tm, d), row), pl.BlockSpec((1, D_MOBA), fix),
                  pl.BlockSpec((D_SWA + D_MOBA, d), fix), pl.BlockSpec((1, 1, d), bat),
                  pl.BlockSpec((1, d), fix), pl.BlockSpec((1, 1, d), bat), pl.BlockSpec((1, 1, d), bat),
                  pl.BlockSpec((d, 2 * LANES), fix), pl.BlockSpec((1, LANES), fix)],
        out_specs=(pl.BlockSpec((tm, d), row), pl.BlockSpec((tm, d), row), pl.BlockSpec((tm, LANES), row)),
        compiler_params=_cparams(("arbitrary",)),
        name="oproj",
    )(ya, yb, x2, gb, wo_bf, g1, n2g, sc2, sh2, wr2, br)


def _route_kernel(lg_ref, ri_ref, rf_ref, cnt_ref, carry_sc):
    i = pl.program_id(0)

    @pl.when(i == 0)
    def _init():
        carry_sc[...] = jnp.zeros_like(carry_sc)

    lg = lg_ref[...]
    tm = lg.shape[0]
    lane = lax.broadcasted_iota(I32, (tm, LANES), 1)
    ninf = -jnp.inf
    gl = jnp.where(lane < N_GROUPS, lg, ninf)
    ge = jnp.exp(gl - jnp.max(gl, axis=-1, keepdims=True))
    gp = ge / jnp.sum(ge, axis=-1, keepdims=True)
    p_grp = jnp.max(gp, axis=-1, keepdims=True)
    g_idx = jnp.min(jnp.where(gp == p_grp, lane, LANES), axis=-1, keepdims=True)
    lo = N_GROUPS + g_idx * EXPERTS_PER_GROUP
    in_grp = jnp.where(lane >= lo, jnp.where(lane < lo + EXPERTS_PER_GROUP, 1, 0), 0) > 0
    el = jnp.where(in_grp, lg, ninf)
    v0 = jnp.max(el, axis=-1, keepdims=True)
    i0 = jnp.min(jnp.where(el == v0, lane, LANES), axis=-1, keepdims=True)
    el = jnp.where(lane == i0, ninf, el)
    v1 = jnp.max(el, axis=-1, keepdims=True)
    i1 = jnp.min(jnp.where(el == v1, lane, LANES), axis=-1, keepdims=True)
    e0 = i0 - N_GROUPS
    e1 = i1 - N_GROUPS
    ex1 = jnp.exp(v1 - v0)
    den = 1.0 + ex1
    w0 = p_grp * (1.0 / den)
    w1 = p_grp * (ex1 / den)
    onehot = jnp.where(lane == e0, 1.0, jnp.where(lane == e1, 1.0, 0.0))
    r_i = lax.broadcasted_iota(I32, (tm, tm), 0)
    c_i = lax.broadcasted_iota(I32, (tm, tm), 1)
    tri = jnp.where(r_i > c_i, 1.0, 0.0).astype(BF16)
    cum = jnp.dot(tri, onehot.astype(BF16), preferred_element_type=F32) + carry_sc[...]
    pos0 = jnp.sum(jnp.where(lane == e0, cum, 0.0), axis=-1, keepdims=True).astype(I32)
    pos1 = jnp.sum(jnp.where(lane == e1, cum, 0.0), axis=-1, keepdims=True).astype(I32)
    carry_sc[...] = carry_sc[...] + jnp.sum(onehot, axis=0, keepdims=True)
    cnt_ref[...] = carry_sc[...]
    ri_ref[...] = jnp.where(lane == 0, e0, jnp.where(lane == 1, e1,
                            jnp.where(lane == 2, pos0, jnp.where(lane == 3, pos1, 0))))
    rf_ref[...] = jnp.where(lane == 0, w0, jnp.where(lane == 1, w1, 0.0))


def _route(logits):
    t = logits.shape[0]
    tm = ROUTE_ROW_TILE
    return pl.pallas_call(
        _route_kernel,
        out_shape=(jax.ShapeDtypeStruct((t, LANES), I32), jax.ShapeDtypeStruct((t, LANES), F32),
                   jax.ShapeDtypeStruct((1, LANES), F32)),
        grid=(t // tm,),
        in_specs=[pl.BlockSpec((tm, LANES), lambda i: (i, 0))],
        out_specs=(pl.BlockSpec((tm, LANES), lambda i: (i, 0)), pl.BlockSpec((tm, LANES), lambda i: (i, 0)),
                   pl.BlockSpec((1, LANES), lambda i: (0, 0))),
        scratch_shapes=[pltpu.VMEM((1, LANES), F32)],
        compiler_params=_cparams(("arbitrary",)),
        name="route",
    )(logits)


def _row_copy(src, dst, sem):
    return pltpu.make_async_copy(src, dst, sem)


def _dispatch_kernel(d0_ref, d1_ref, h_ref, xin_ref, xg_ref, sem):
    del xin_ref
    tm = h_ref.shape[0]
    base = pl.program_id(0) * tm

    def issue(r, carry):
        src = h_ref.at[pl.ds(r, 1), :]
        _row_copy(src, xg_ref.at[pl.ds(d0_ref[base + r], 1), :], sem.at[0]).start(priority=0)
        _row_copy(src, xg_ref.at[pl.ds(d1_ref[base + r], 1), :], sem.at[0]).start(priority=1)
        return carry
    lax.fori_loop(0, tm, issue, 0, unroll=DMA_UNROLL)

    def drain(r, carry):
        src = h_ref.at[pl.ds(r, 1), :]
        _row_copy(src, xg_ref.at[pl.ds(d0_ref[base + r], 1), :], sem.at[0]).wait()
        _row_copy(src, xg_ref.at[pl.ds(d1_ref[base + r], 1), :], sem.at[0]).wait()
        return carry
    lax.fori_loop(0, tm, drain, 0, unroll=DMA_UNROLL)


def _dispatch(d0, d1, h2, xg):
    t, d = h2.shape
    tm = MOVE_ROW_TILE
    return pl.pallas_call(
        _dispatch_kernel,
        out_shape=jax.ShapeDtypeStruct(xg.shape, xg.dtype),
        grid_spec=pltpu.PrefetchScalarGridSpec(
            num_scalar_prefetch=2, grid=(t // tm,),
            in_specs=[pl.BlockSpec((tm, d), lambda i, a, b: (i, 0)),
                      pl.BlockSpec(memory_space=pl.ANY)],
            out_specs=pl.BlockSpec(memory_space=pl.ANY),
            scratch_shapes=[pltpu.SemaphoreType.DMA((1,))]),
        input_output_aliases={3: 0},
        compiler_params=_cparams(("arbitrary",)),
        name="dispatch",
    )(d0, d1, h2, xg)


def _ffn_kernel(te_ref, nu_ref, x_ref, wg_ref, wu_ref, wd_ref, y_ref, wg_sc, wu_sc, wd_sc):
    i = pl.program_id(0)
    new_expert = jnp.logical_or(i == 0, te_ref[i] != te_ref[jnp.maximum(i - 1, 0)])

    @pl.when(new_expert)
    def _cast():
        wg_sc[...] = wg_ref[0].astype(BF16)
        wu_sc[...] = wu_ref[0].astype(BF16)
        wd_sc[...] = wd_ref[0].astype(BF16)

    @pl.when(i < nu_ref[0])
    def _compute():
        xb = x_ref[...].astype(BF16)
        g = jnp.dot(xb, wg_sc[...], preferred_element_type=F32)
        u = jnp.dot(xb, wu_sc[...], preferred_element_type=F32)
        hid = (_silu(g) * u).astype(BF16)
        y_ref[...] = jnp.dot(hid, wd_sc[...], preferred_element_type=F32)

    @pl.when(i >= nu_ref[0])
    def _skip():
        y_ref[...] = jnp.zeros_like(y_ref)


def _expert_ffn(tile_expert, n_used, xg, wg, wu, wd, layer):
    npad, d = xg.shape
    tr = EXPERT_ROW_TILE
    e_base = layer * N_EXPERTS
    wmap = lambda i, te, nu: (e_base + te[i], 0, 0)
    return pl.pallas_call(
        _ffn_kernel,
        out_shape=jax.ShapeDtypeStruct((npad, d), F32),
        grid_spec=pltpu.PrefetchScalarGridSpec(
            num_scalar_prefetch=2, grid=(npad // tr,),
            in_specs=[pl.BlockSpec((tr, d), lambda i, te, nu: (jnp.minimum(i, nu[0] - 1), 0)),
                      pl.BlockSpec((1, d, D_EXPERT), wmap),
                      pl.BlockSpec((1, d, D_EXPERT), wmap),
                      pl.BlockSpec((1, D_EXPERT, d), wmap)],
            out_specs=pl.BlockSpec((tr, d), lambda i, te, nu: (i, 0)),
            scratch_shapes=[pltpu.VMEM((d, D_EXPERT), BF16), pltpu.VMEM((d, D_EXPERT), BF16),
                            pltpu.VMEM((D_EXPERT, d), BF16)]),
        compiler_params=_cparams(("arbitrary",)),
        name="expert_ffn",
    )(tile_expert, n_used, xg, wg, wu, wd)


def _combine_kernel(d0_ref, d1_ref, x_ref, rf_ref, g2_ref, fg_ref, y_ref, o_ref, buf, sem, *, final):
    tm = x_ref.shape[0]
    base = pl.program_id(0) * tm

    def issue(r, carry):
        _row_copy(y_ref.at[pl.ds(d0_ref[base + r], 1), :], buf.at[0, pl.ds(r, 1), :], sem.at[0]).start(priority=0)
        _row_copy(y_ref.at[pl.ds(d1_ref[base + r], 1), :], buf.at[1, pl.ds(r, 1), :], sem.at[0]).start(priority=1)
        return carry
    lax.fori_loop(0, tm, issue, 0, unroll=DMA_UNROLL)

    def drain(r, carry):
        _row_copy(y_ref.at[pl.ds(d0_ref[base + r], 1), :], buf.at[0, pl.ds(r, 1), :], sem.at[0]).wait()
        _row_copy(y_ref.at[pl.ds(d1_ref[base + r], 1), :], buf.at[1, pl.ds(r, 1), :], sem.at[0]).wait()
        return carry
    lax.fori_loop(0, tm, drain, 0, unroll=DMA_UNROLL)

    w0 = rf_ref[:, 0:1]
    w1 = rf_ref[:, 1:2]
    xn = x_ref[...] + g2_ref[0] * (w0 * buf[0] + w1 * buf[1])
    if final:
        xn = (xn * _rms(xn)) * fg_ref[...]
    o_ref[...] = xn


def _combine(d0, d1, x2, rf, g2, final_g, y, seq, final):
    t, d = x2.shape
    tm = MOVE_ROW_TILE
    per_b = seq // tm
    return pl.pallas_call(
        functools.partial(_combine_kernel, final=final),
        out_shape=jax.ShapeDtypeStruct((t, d), F32),
        grid_spec=pltpu.PrefetchScalarGridSpec(
            num_scalar_prefetch=2, grid=(t // tm,),
            in_specs=[pl.BlockSpec((tm, d), lambda i, a, b: (i, 0)),
                      pl.BlockSpec((tm, LANES), lambda i, a, b: (i, 0)),
                      pl.BlockSpec((1, 1, d), lambda i, a, b: (i // per_b, 0, 0)),
                      pl.BlockSpec((1, d), lambda i, a, b: (0, 0)),
                      pl.BlockSpec(memory_space=pl.ANY)],
            out_specs=pl.BlockSpec((tm, d), lambda i, a, b: (i, 0)),
            scratch_shapes=[pltpu.VMEM((2, tm, d), F32), pltpu.SemaphoreType.DMA((1,))]),
        compiler_params=_cparams(("arbitrary",)),
        name="combine",
    )(d0, d1, x2, rf, g2, final_g, y)


def _routing_plan(ri, cnt, n_tiles):
    tr = EXPERT_ROW_TILE
    counts = cnt[0, :N_EXPERTS].astype(I32)
    tiles_e = (counts + tr - 1) // tr
    tile_end = jnp.cumsum(tiles_e)
    row_off = (tile_end - tiles_e) * tr
    d0 = jnp.take(row_off, ri[:, 0]) + ri[:, 2]
    d1 = jnp.take(row_off, ri[:, 1]) + ri[:, 3]
    n_used = tile_end[-1]
    tid = jnp.minimum(jnp.arange(n_tiles, dtype=I32), n_used - 1)
    tile_expert = jnp.sum((tid[:, None] >= tile_end[None, :]).astype(I32), axis=1)
    return d0, d1, tile_expert.astype(I32), n_used.reshape(1).astype(I32)


def kernel(x, c, rel_bias, w_ada, b_ada, norm1_g, norm2_g, w_qkv, attn_sinks, swa_out_g, moba_out_g,
           w_o, w_group, b_group, w_expert_router, b_expert_router, w_gate, w_up, w_down, final_g):
    b, s, d = x.shape
    depth = w_ada.shape[0]
    t = b * s
    nblk = s // MOBA_BLOCK

    w = SWA_WINDOW
    key = jnp.arange(2 * w, dtype=I32)[:, None]
    swa_dist = (w + jnp.arange(w, dtype=I32))[None, :] - key
    band = (swa_dist >= 0) & (swa_dist < SWA_WINDOW)
    first = jnp.stack([band & (key >= w), band])
    swa_bkt = jnp.where(first, _rel_bucket(swa_dist)[None], REL_BUCKETS)
    n_tab = min(nblk, FAR_TILE + 1)
    pos = jnp.arange(MOBA_BLOCK, dtype=I32)
    moba_dist = (jnp.arange(n_tab, dtype=I32)[:, None, None] * MOBA_BLOCK
                 + pos[None, None, :] - pos[None, :, None])
    moba_bkt = jnp.where(moba_dist >= 0, _rel_bucket(moba_dist), REL_BUCKETS)
    swa_tab = _bias_table(rel_bias, swa_bkt, SWA_Q_HEADS, 0, scale=LOG2E, group=SWA_GROUP)
    moba_tab = _bias_table(rel_bias, moba_bkt, MOBA_HEADS, SWA_Q_HEADS, scale=LOG2E)

    c_pad = jnp.zeros((8, d), F32).at[:b].set(c)
    mod = _ada_mod(c_pad, w_ada, b_ada)[:, :b]

    w_qkv_bf = w_qkv.astype(BF16)
    w_o_bf = w_o.astype(BF16)
    wg_all = w_gate.reshape(depth * N_EXPERTS, d, D_EXPERT)
    wu_all = w_up.reshape(depth * N_EXPERTS, d, D_EXPERT)
    wd_all = w_down.reshape(depth * N_EXPERTS, D_EXPERT, d)

    pad = jnp.zeros((depth, d, LANES - N_GROUPS - N_EXPERTS), F32)
    wr = jnp.concatenate([w_group, w_expert_router, pad], axis=2)
    wr_hi = wr.astype(BF16)
    wr2_all = jnp.concatenate([wr_hi, (wr - wr_hi.astype(F32)).astype(BF16)], axis=2)
    br_all = jnp.concatenate([b_group, b_expert_router, pad[:, 0, :]], axis=1).reshape(depth, 1, LANES)

    n_tiles = (t * 2) // EXPERT_ROW_TILE + N_EXPERTS
    xg = jnp.zeros((n_tiles * EXPERT_ROW_TILE, d), F32)
    x2 = x.reshape(t, d)
    fg = final_g.reshape(1, d)
    for l in range(depth):
        sh1, sc1, g1, sh2, sc2, g2 = [mod[l, :, k * d:(k + 1) * d].reshape(b, 1, d) for k in range(6)]
        qkv = _qkv_proj(x2, sh1, sc1, norm1_g[l].reshape(1, d), w_qkv_bf[l], s)
        qkv3 = qkv.reshape(b, s, D_QKV)
        ya = _swa_attention(qkv3, attn_sinks[l], swa_tab, swa_out_g[l].reshape(1, D_SWA))
        yb = _moba_attention(qkv3, moba_tab)
        x2, h2, logits = _oproj(ya.reshape(t, D_SWA), yb.reshape(t, D_MOBA), x2,
                                moba_out_g[l].reshape(1, D_MOBA), w_o_bf[l], g1,
                                norm2_g[l].reshape(1, d), sc2, sh2, wr2_all[l], br_all[l], s)
        ri, rf, cnt = _route(logits)
        d0, d1, tile_expert, n_used = _routing_plan(ri, cnt, n_tiles)
        xg = _dispatch(d0, d1, h2, xg)
        y = _expert_ffn(tile_expert, n_used, xg, wg_all, wu_all, wd_all, l)
        x2 = _combine(d0, d1, x2, rf, g2, fg, y, s, final=(l == depth - 1))
    return x2.reshape(b, s, d)
```

```python
import functools
import math

import jax
import jax.numpy as jnp
from jax import lax
from jax.experimental import pallas as pl
from jax.experimental.pallas import tpu as pltpu

F32 = jnp.float32
BF16 = jnp.bfloat16
I32 = jnp.int32
HIGHEST = lax.Precision.HIGHEST

D_MODEL = 2048
HEAD_DIM = 64
SWA_Q_HEADS = 16
SWA_KV_HEADS = 2
SWA_GROUP = SWA_Q_HEADS // SWA_KV_HEADS
SWA_WINDOW = 128
MOBA_HEADS = 16
MOBA_BLOCK = 256
MOBA_TOPK = 3
D_SWA = SWA_Q_HEADS * HEAD_DIM
D_SWA_KV = SWA_KV_HEADS * HEAD_DIM
D_MOBA = MOBA_HEADS * HEAD_DIM
D_QKV = D_SWA + 2 * D_SWA_KV + 3 * D_MOBA
REL_BUCKETS = 32
REL_MAX_DISTANCE = 2048
N_GROUPS = 4
EXPERTS_PER_GROUP = 8
N_EXPERTS = N_GROUPS * EXPERTS_PER_GROUP
D_EXPERT = 512
EPS = 1e-6
NEG = -1e30
SCALE = 1.0 / math.sqrt(HEAD_DIM)
LOG2E = math.log2(math.e)

LANES = 128
BF16_ROWS = 16
V_ROWS = HEAD_DIM + BF16_ROWS
MOBA_STEP_HEADS = 4
FAR_TILE = REL_MAX_DISTANCE // MOBA_BLOCK + 1
QKV_COL_TILE = D_QKV // 2
QKV_ROW_TILE = 1024
OPROJ_ROW_TILE = 512
ROUTE_ROW_TILE = 512
MOVE_ROW_TILE = 1024
EXPERT_ROW_TILE = 256
ADA_COL_TILE = 1024
DMA_UNROLL = 8
VMEM_LIMIT = 56 * 1024 * 1024


def _cparams(sem, vmem=VMEM_LIMIT):
    return pltpu.CompilerParams(dimension_semantics=sem, vmem_limit_bytes=vmem)


def _rel_bucket(dist):
    n = jnp.maximum(dist, 0)
    max_exact = REL_BUCKETS // 2
    nf = jnp.maximum(n, 1).astype(F32)
    large = max_exact + (jnp.log(nf / max_exact) / math.log(REL_MAX_DISTANCE / max_exact)
                         * (REL_BUCKETS - max_exact)).astype(I32)
    large = jnp.minimum(large, REL_BUCKETS - 1)
    return jnp.where(n < max_exact, n, large)


def _silu(v):
    return v * (1.0 / (1.0 + jnp.exp(-v)))


def _ada_kernel(c_ref, w_ref, b_ref, o_ref):
    ca = _silu(c_ref[...])
    o_ref[0] = jnp.dot(ca, w_ref[0], precision=HIGHEST, preferred_element_type=F32) + b_ref[0]


def _ada_mod(c_pad, w_ada, b_ada):
    depth, d, n = w_ada.shape
    rows = c_pad.shape[0]
    return pl.pallas_call(
        _ada_kernel,
        out_shape=jax.ShapeDtypeStruct((depth, rows, n), F32),
        grid=(depth, n // ADA_COL_TILE),
        in_specs=[pl.BlockSpec((rows, d), lambda l, j: (0, 0)),
                  pl.BlockSpec((1, d, ADA_COL_TILE), lambda l, j: (l, 0, j)),
                  pl.BlockSpec((1, 1, ADA_COL_TILE), lambda l, j: (l, 0, j))],
        out_specs=pl.BlockSpec((1, rows, ADA_COL_TILE), lambda l, j: (l, 0, j)),
        compiler_params=_cparams(("arbitrary", "arbitrary")),
        name="ada_mod",
    )(c_pad, w_ada, b_ada.reshape(depth, 1, n))


def _bias_kernel(rb_ref, bkt_ref, o_ref, *, head0, scale):
    h = pl.program_id(0) + head0
    b = bkt_ref[0]
    acc = jnp.full(b.shape, NEG, F32)
    for k in range(REL_BUCKETS):
        acc = jnp.where(b == k, rb_ref[k, h] * scale, acc)
    o_ref[0, 0] = acc


def _bias_table(rel_bias, bkt, n_heads, head0, scale=1.0, group=1):
    nt, r, c = bkt.shape
    return pl.pallas_call(
        functools.partial(_bias_kernel, head0=head0, scale=scale),
        out_shape=jax.ShapeDtypeStruct((n_heads // group, nt, r, c * group), F32),
        grid=(n_heads, nt),
        in_specs=[pl.BlockSpec(memory_space=pltpu.SMEM),
                  pl.BlockSpec((1, r, c), lambda h, t: (t, 0, 0))],
        out_specs=pl.BlockSpec((1, 1, r, c), lambda h, t: (h // group, t, 0, h % group)),
        compiler_params=_cparams(("arbitrary", "arbitrary")),
        name="bias_table",
    )(rel_bias, bkt)


def _rms(v):
    return lax.rsqrt(jnp.mean(v * v, axis=-1, keepdims=True) + EPS)


def _qkv_kernel(x_ref, sh_ref, sc_ref, g_ref, w_ref, o_ref):
    x = x_ref[...]
    h = (x * _rms(x)) * g_ref[...] * (1.0 + sc_ref[0]) + sh_ref[0]
    o_ref[...] = jnp.dot(h.astype(BF16), w_ref[...], preferred_element_type=F32).astype(BF16)


def _qkv_proj(x2, sh, sc, g, w_bf, seq):
    t, d = x2.shape
    tm = QKV_ROW_TILE
    per_b = seq // tm
    return pl.pallas_call(
        _qkv_kernel,
        out_shape=jax.ShapeDtypeStruct((t, D_QKV), BF16),
        grid=(D_QKV // QKV_COL_TILE, t // tm),
        in_specs=[pl.BlockSpec((tm, d), lambda j, i: (i, 0)),
                  pl.BlockSpec((1, 1, d), lambda j, i: (i // per_b, 0, 0)),
                  pl.BlockSpec((1, 1, d), lambda j, i: (i // per_b, 0, 0)),
                  pl.BlockSpec((1, d), lambda j, i: (0, 0)),
                  pl.BlockSpec((d, QKV_COL_TILE), lambda j, i: (0, j))],
        out_specs=pl.BlockSpec((tm, QKV_COL_TILE), lambda j, i: (i, j)),
        compiler_params=_cparams(("arbitrary", "arbitrary")),
        name="qkv_proj",
    )(x2, sh, sc, g, w_bf)


def _swa_kernel(sink_ref, q_ref, kvc_ref, kvp_ref, bias_ref, g_ref, o_ref):
    n = pl.program_id(1)
    w = SWA_WINDOW
    tsel = jnp.minimum(n, 1)
    qt = (q_ref[0].astype(F32) * (SCALE * LOG2E)).T.astype(BF16)
    kv = jnp.concatenate([kvp_ref[0], kvc_ref[0]], axis=0)
    k_all = kv[:, :D_SWA_KV]
    vt = kv[:, D_SWA_KV:].astype(F32).T.astype(BF16)
    zeros = jnp.zeros((HEAD_DIM, SWA_GROUP * w), BF16)
    scores = []
    for kh in range(SWA_KV_HEADS):
        h0 = kh * SWA_GROUP
        q_kh = jnp.concatenate([qt[(h0 + g) * HEAD_DIM:(h0 + g + 1) * HEAD_DIM, :]
                                for g in range(SWA_GROUP)], axis=1)
        rhs = jnp.concatenate([q_kh if j == kh else zeros for j in range(SWA_KV_HEADS)], axis=0)
        scores.append(jnp.dot(k_all, rhs, preferred_element_type=F32) + bias_ref[kh, tsel])
    outs = []
    for kh in range(SWA_KV_HEADS):
        h0 = kh * SWA_GROUP
        st = scores[kh]
        sink = jnp.concatenate([jnp.full((1, w), sink_ref[h0 + g] * LOG2E, F32)
                                for g in range(SWA_GROUP)], axis=1)
        m = jnp.maximum(jnp.max(st, axis=0, keepdims=True), sink)
        e = jnp.exp2(st - m)
        denom = jnp.sum(e, axis=0, keepdims=True) + jnp.exp2(sink - m)
        pv = jnp.dot(vt[kh * HEAD_DIM:(kh + 1) * HEAD_DIM, :], e.astype(BF16),
                     preferred_element_type=F32)
        o = pv * (1.0 / denom)
        outs.extend(o[:, g * w:(g + 1) * w] for g in range(SWA_GROUP))
    y = jnp.concatenate(outs, axis=0).T
    o_ref[0] = ((y * _rms(y)) * g_ref[...]).astype(BF16)


def _swa_attention(qkv3, sinks, bias_tab, out_g):
    b, s, _ = qkv3.shape
    w = SWA_WINDOW
    kv_blk = D_SWA // (2 * D_SWA_KV)
    return pl.pallas_call(
        _swa_kernel,
        out_shape=jax.ShapeDtypeStruct((b, s, D_SWA), BF16),
        grid=(b, s // w),
        in_specs=[pl.BlockSpec(memory_space=pltpu.SMEM),
                  pl.BlockSpec((1, w, D_SWA), lambda bi, n: (bi, n, 0)),
                  pl.BlockSpec((1, w, 2 * D_SWA_KV), lambda bi, n: (bi, n, kv_blk)),
                  pl.BlockSpec((1, w, 2 * D_SWA_KV), lambda bi, n: (bi, jnp.maximum(n - 1, 0), kv_blk)),
                  pl.BlockSpec((SWA_KV_HEADS, 2, 2 * w, SWA_GROUP * w), lambda bi, n: (0, 0, 0, 0)),
                  pl.BlockSpec((1, D_SWA), lambda bi, n: (0, 0))],
        out_specs=pl.BlockSpec((1, w, D_SWA), lambda bi, n: (bi, n, 0)),
        compiler_params=_cparams(("arbitrary", "arbitrary")),
        name="swa_attention",
    )(sinks, qkv3, qkv3, qkv3, bias_tab, out_g)


def _moba_kernel(q_ref, k_ref, v_ref, bias_ref, o_ref,
                 vt_sc, kmean_sc, rhs_sc, sel_sc, s_sc, p_sc, mb_sc, al_sc, al2_sc, m_sc, acc_sc, *, nblk):
    c = pl.program_id(2)
    blk = MOBA_BLOCK
    nh = MOBA_STEP_HEADS
    far = bias_ref.shape[1] - 1

    @pl.when(c == 0)
    def _prep():
        def body(j, carry):
            off = pl.multiple_of(j * blk, blk)
            vt = v_ref[0, pl.ds(off, blk), :].astype(F32).T.astype(BF16)
            ones = jnp.ones((BF16_ROWS, blk), BF16)
            parts = []
            for hh in range(nh):
                parts += [vt[hh * HEAD_DIM:(hh + 1) * HEAD_DIM], ones]
            vt_sc[j] = jnp.concatenate(parts, axis=0)
            kb = k_ref[0, pl.ds(off, blk), :].astype(F32)
            kmean_sc[pl.ds(j, 1), :] = jnp.sum(kb, axis=0, keepdims=True) * (1.0 / blk)
            return carry
        lax.fori_loop(0, nblk, body, 0)

    qt = q_ref[0].astype(F32).T
    row = lax.broadcasted_iota(I32, (nh * HEAD_DIM, blk), 0)
    blk_i = lax.broadcasted_iota(I32, (nblk, blk), 0)
    for hh in range(nh):
        in_head = jnp.where(row >= hh * HEAD_DIM, jnp.where(row < (hh + 1) * HEAD_DIM, 1, 0), 0) > 0
        qt_h = jnp.where(in_head, qt, 0.0)
        rhs_sc[hh] = (qt_h * (SCALE * LOG2E)).astype(BF16)
        gate = jnp.dot(kmean_sc[...], qt_h, precision=HIGHEST, preferred_element_type=F32)
        gate = jnp.where(blk_i < c, gate, NEG)
        sel = jnp.where(blk_i == c, 1.0, 0.0)
        for _ in range(min(MOBA_TOPK, nblk)):
            mx = jnp.max(gate, axis=0, keepdims=True)
            idx = jnp.min(jnp.where(gate == mx, blk_i, nblk), axis=0, keepdims=True)
            hit = blk_i == idx
            sel = jnp.where(hit, jnp.where(blk_i < c, 1.0, sel), sel)
            gate = jnp.where(hit, -jnp.inf, gate)
        sel_sc[hh] = sel
        m_sc[hh] = jnp.full((1, blk), NEG, F32)
        acc_sc[hh] = jnp.zeros((V_ROWS, blk), F32)

    def stage_a(u):
        m = [m_sc[hh] for hh in range(nh)]
        for ti in range(2):
            jt = 2 * u + ti
            off = pl.multiple_of(jt * blk, blk)
            kj = k_ref[0, pl.ds(off, blk), :]
            d = jnp.clip(c - jt, 0, far)
            for hh in range(nh):
                st = jnp.dot(kj, rhs_sc[hh], preferred_element_type=F32) + bias_ref[hh, d]
                s_sc[ti, hh] = st
                chosen = sel_sc[hh, pl.ds(jt, 1), :] > 0.5
                m_big = jnp.maximum(m[hh], jnp.max(st, axis=0, keepdims=True))
                m_new = jnp.where(chosen, m_big, m[hh])
                mb_sc[ti, hh] = m_big
                al_sc[ti, hh] = jnp.exp2(m[hh] - m_new)
                m[hh] = m_new
        for hh in range(nh):
            m_sc[hh] = m[hh]

    def stage_b_exp():
        for ti in range(2):
            for hh in range(nh):
                p_sc[ti, hh] = jnp.exp2(s_sc[ti, hh] - mb_sc[ti, hh]).astype(BF16)

    def stage_b_pv(u):
        for ti in range(2):
            jt = 2 * u + ti
            for hh in range(nh):
                chosen = sel_sc[hh, pl.ds(jt, 1), :] > 0.5
                pv = jnp.dot(vt_sc[jt, pl.ds(hh * V_ROWS, V_ROWS), :], p_sc[ti, hh],
                             preferred_element_type=F32)
                acc_sc[hh] = al2_sc[ti, hh] * acc_sc[hh] + jnp.where(chosen, pv, 0.0)

    n_pairs = (c + 2) // 2
    stage_a(0)

    def body(u, carry):
        stage_b_exp()
        al2_sc[...] = al_sc[...]
        stage_a(u)
        stage_b_pv(u - 1)
        return carry
    lax.fori_loop(1, n_pairs, body, 0)
    stage_b_exp()
    al2_sc[...] = al_sc[...]
    stage_b_pv(n_pairs - 1)

    outs = []
    for hh in range(nh):
        inv_l = 1.0 / acc_sc[hh, HEAD_DIM:HEAD_DIM + 1, :]
        outs.append(acc_sc[hh, :HEAD_DIM, :] * inv_l)
    o_ref[0] = jnp.concatenate(outs, axis=0).T


def _moba_attention(qkv3, bias_tab):
    b, s, _ = qkv3.shape
    blk = MOBA_BLOCK
    nblk = s // blk
    assert nblk % 2 == 0
    nh = MOBA_STEP_HEADS
    groups = MOBA_HEADS // nh
    pw = nh * HEAD_DIM
    q0 = (D_SWA + 2 * D_SWA_KV) // pw
    k0 = q0 + D_MOBA // pw
    v0 = k0 + D_MOBA // pw
    nt = bias_tab.shape[1]
    return pl.pallas_call(
        functools.partial(_moba_kernel, nblk=nblk),
        out_shape=jax.ShapeDtypeStruct((b, s, D_MOBA), F32),
        grid=(b, groups, nblk),
        in_specs=[pl.BlockSpec((1, blk, pw), lambda bi, hp, c: (bi, c, q0 + hp)),
                  pl.BlockSpec((1, s, pw), lambda bi, hp, c: (bi, 0, k0 + hp)),
                  pl.BlockSpec((1, s, pw), lambda bi, hp, c: (bi, 0, v0 + hp)),
                  pl.BlockSpec((nh, nt, blk, blk), lambda bi, hp, c: (hp, 0, 0, 0))],
        out_specs=pl.BlockSpec((1, blk, pw), lambda bi, hp, c: (bi, c, hp)),
        scratch_shapes=[pltpu.VMEM((nblk, nh * V_ROWS, blk), BF16),
                        pltpu.VMEM((nblk, pw), F32),
                        pltpu.VMEM((nh, pw, blk), BF16),
                        pltpu.VMEM((nh, nblk, blk), F32),
                        pltpu.VMEM((2, nh, blk, blk), F32),
                        pltpu.VMEM((2, nh, blk, blk), BF16),
                        pltpu.VMEM((2, nh, 1, blk), F32),
                        pltpu.VMEM((2, nh, 1, blk), F32),
                        pltpu.VMEM((2, nh, 1, blk), F32),
                        pltpu.VMEM((nh, 1, blk), F32),
                        pltpu.VMEM((nh, V_ROWS, blk), F32)],
        compiler_params=_cparams(("arbitrary", "arbitrary", "arbitrary")),
        name="moba_attention",
    )(qkv3, qkv3, qkv3, bias_tab)


def _oproj_kernel(ya_ref, yb_ref, x_ref, gb_ref, wo_ref, g1_ref, n2g_ref, sc2_ref, sh2_ref,
                  wr2_ref, br_ref, xo_ref, h2_ref, lg_ref):
    yb = yb_ref[...]
    ybn = ((yb * _rms(yb)) * gb_ref[...]).astype(BF16)
    y = (jnp.dot(ya_ref[...], wo_ref[0:D_SWA, :], preferred_element_type=F32)
         + jnp.dot(ybn, wo_ref[D_SWA:D_SWA + D_MOBA, :], preferred_element_type=F32))
    xn = x_ref[...] + g1_ref[0] * y
    xo_ref[...] = xn
    h2 = (xn * _rms(xn)) * n2g_ref[...] * (1.0 + sc2_ref[0]) + sh2_ref[0]
    h2_ref[...] = h2
    hi = h2.astype(BF16)
    lo = (h2 - hi.astype(F32)).astype(BF16)
    both = jnp.dot(hi, wr2_ref[...], preferred_element_type=F32)
    lg_ref[...] = (both[:, :LANES] + both[:, LANES:]
                   + jnp.dot(lo, wr2_ref[:, :LANES], preferred_element_type=F32) + br_ref[...])


def _oproj(ya, yb, x2, gb, wo_bf, g1, n2g, sc2, sh2, wr2, br, seq):
    t, d = x2.shape
    tm = OPROJ_ROW_TILE
    per_b = seq // tm
    row = lambda i: (i, 0)
    fix = lambda i: (0, 0)
    bat = lambda i: (i // per_b, 0, 0)
    return pl.pallas_call(
        _oproj_kernel,
        out_shape=(jax.ShapeDtypeStruct((t, d), F32), jax.ShapeDtypeStruct((t, d), F32),
                   jax.ShapeDtypeStruct((t, LANES), F32)),
        grid=(t // tm,),
        in_specs=[pl.BlockSpec((tm, D_SWA), row), pl.BlockSpec((tm, D_MOBA), row),
                  pl.BlockSpec((tm, d), row), pl.BlockSpec((1, D_MOBA), fix),
                  pl.BlockSpec((D_SWA + D_MOBA, d), fix), pl.BlockSpec((1, 1, d), bat),
                  pl.BlockSpec((1, d), fix), pl.BlockSpec((1, 1, d), bat), pl.BlockSpec((1, 1, d), bat),
                  pl.BlockSpec((d, 2 * LANES), fix), pl.BlockSpec((1, LANES), fix)],
        out_specs=(pl.BlockSpec((tm, d), row), pl.BlockSpec((tm, d), row), pl.BlockSpec((tm, LANES), row)),
        compiler_params=_cparams(("arbitrary",)),
        name="oproj",
    )(ya, yb, x2, gb, wo_bf, g1, n2g, sc2, sh2, wr2, br)


def _route_kernel(lg_ref, ri_ref, rf_ref, cnt_ref, carry_sc):
    i = pl.program_id(0)

    @pl.when(i == 0)
    def _init():
        carry_sc[...] = jnp.zeros_like(carry_sc)

    lg = lg_ref[...]
    tm = lg.shape[0]
    lane = lax.broadcasted_iota(I32, (tm, LANES), 1)
    ninf = -jnp.inf
    gl = jnp.where(lane < N_GROUPS, lg, ninf)
    ge = jnp.exp(gl - jnp.max(gl, axis=-1, keepdims=True))
    gp = ge / jnp.sum(ge, axis=-1, keepdims=True)
    p_grp = jnp.max(gp, axis=-1, keepdims=True)
    g_idx = jnp.min(jnp.where(gp == p_grp, lane, LANES), axis=-1, keepdims=True)
    lo = N_GROUPS + g_idx * EXPERTS_PER_GROUP
    in_grp = jnp.where(lane >= lo, jnp.where(lane < lo + EXPERTS_PER_GROUP, 1, 0), 0) > 0
    el = jnp.where(in_grp, lg, ninf)
    v0 = jnp.max(el, axis=-1, keepdims=True)
    i0 = jnp.min(jnp.where(el == v0, lane, LANES), axis=-1, keepdims=True)
    el = jnp.where(lane == i0, ninf, el)
    v1 = jnp.max(el, axis=-1, keepdims=True)
    i1 = jnp.min(jnp.where(el == v1, lane, LANES), axis=-1, keepdims=True)
    e0 = i0 - N_GROUPS
    e1 = i1 - N_GROUPS
    ex1 = jnp.exp(v1 - v0)
    den = 1.0 + ex1
    w0 = p_grp * (1.0 / den)
    w1 = p_grp * (ex1 / den)
    onehot = jnp.where(lane == e0, 1.0, jnp.where(lane == e1, 1.0, 0.0))
    r_i = lax.broadcasted_iota(I32, (tm, tm), 0)
    c_i = lax.broadcasted_iota(I32, (tm, tm), 1)
    tri = jnp.where(r_i > c_i, 1.0, 0.0).astype(BF16)
    cum = jnp.dot(tri, onehot.astype(BF16), preferred_element_type=F32) + carry_sc[...]
    pos0 = jnp.sum(jnp.where(lane == e0, cum, 0.0), axis=-1, keepdims=True).astype(I32)
    pos1 = jnp.sum(jnp.where(lane == e1, cum, 0.0), axis=-1, keepdims=True).astype(I32)
    carry_sc[...] = carry_sc[...] + jnp.sum(onehot, axis=0, keepdims=True)
    cnt_ref[...] = carry_sc[...]
    ri_ref[...] = jnp.where(lane == 0, e0, jnp.where(lane == 1, e1,
                            jnp.where(lane == 2, pos0, jnp.where(lane == 3, pos1, 0))))
    rf_ref[...] = jnp.where(lane == 0, w0, jnp.where(lane == 1, w1, 0.0))


def _route(logits):
    t = logits.shape[0]
    tm = ROUTE_ROW_TILE
    return pl.pallas_call(
        _route_kernel,
        out_shape=(jax.ShapeDtypeStruct((t, LANES), I32), jax.ShapeDtypeStruct((t, LANES), F32),
                   jax.ShapeDtypeStruct((1, LANES), F32)),
        grid=(t // tm,),
        in_specs=[pl.BlockSpec((tm, LANES), lambda i: (i, 0))],
        out_specs=(pl.BlockSpec((tm, LANES), lambda i: (i, 0)), pl.BlockSpec((tm, LANES), lambda i: (i, 0)),
                   pl.BlockSpec((1, LANES), lambda i: (0, 0))),
        scratch_shapes=[pltpu.VMEM((1, LANES), F32)],
        compiler_params=_cparams(("arbitrary",)),
        name="route",
    )(logits)


def _row_copy(src, dst, sem):
    return pltpu.make_async_copy(src, dst, sem)


def _dispatch_kernel(d0_ref, d1_ref, h_ref, xin_ref, xg_ref, sem):
    del xin_ref
    tm = h_ref.shape[0]
    base = pl.program_id(0) * tm

    def issue(r, carry):
        src = h_ref.at[pl.ds(r, 1), :]
        _row_copy(src, xg_ref.at[pl.ds(d0_ref[base + r], 1), :], sem.at[0]).start(priority=0)
        _row_copy(src, xg_ref.at[pl.ds(d1_ref[base + r], 1), :], sem.at[0]).start(priority=1)
        return carry
    lax.fori_loop(0, tm, issue, 0, unroll=DMA_UNROLL)

    def drain(r, carry):
        src = h_ref.at[pl.ds(r, 1), :]
        _row_copy(src, xg_ref.at[pl.ds(d0_ref[base + r], 1), :], sem.at[0]).wait()
        _row_copy(src, xg_ref.at[pl.ds(d1_ref[base + r], 1), :], sem.at[0]).wait()
        return carry
    lax.fori_loop(0, tm, drain, 0, unroll=DMA_UNROLL)


def _dispatch(d0, d1, h2, xg):
    t, d = h2.shape
    tm = MOVE_ROW_TILE
    return pl.pallas_call(
        _dispatch_kernel,
        out_shape=jax.ShapeDtypeStruct(xg.shape, xg.dtype),
        grid_spec=pltpu.PrefetchScalarGridSpec(
            num_scalar_prefetch=2, grid=(t // tm,),
            in_specs=[pl.BlockSpec((tm, d), lambda i, a, b: (i, 0)),
                      pl.BlockSpec(memory_space=pl.ANY)],
            out_specs=pl.BlockSpec(memory_space=pl.ANY),
            scratch_shapes=[pltpu.SemaphoreType.DMA((1,))]),
        input_output_aliases={3: 0},
        compiler_params=_cparams(("arbitrary",)),
        name="dispatch",
    )(d0, d1, h2, xg)


def _ffn_kernel(te_ref, nu_ref, x_ref, wg_ref, wu_ref, wd_ref, y_ref, wg_sc, wu_sc, wd_sc):
    i = pl.program_id(0)
    new_expert = jnp.logical_or(i == 0, te_ref[i] != te_ref[jnp.maximum(i - 1, 0)])

    @pl.when(new_expert)
    def _cast():
        wg_sc[...] = wg_ref[0].astype(BF16)
        wu_sc[...] = wu_ref[0].astype(BF16)
        wd_sc[...] = wd_ref[0].astype(BF16)

    @pl.when(i < nu_ref[0])
    def _compute():
        xb = x_ref[...].astype(BF16)
        g = jnp.dot(xb, wg_sc[...], preferred_element_type=F32)
        u = jnp.dot(xb, wu_sc[...], preferred_element_type=F32)
        hid = (_silu(g) * u).astype(BF16)
        y_ref[...] = jnp.dot(hid, wd_sc[...], preferred_element_type=F32)

    @pl.when(i >= nu_ref[0])
    def _skip():
        y_ref[...] = jnp.zeros_like(y_ref)


def _expert_ffn(tile_expert, n_used, xg, wg, wu, wd, layer):
    npad, d = xg.shape
    tr = EXPERT_ROW_TILE
    e_base = layer * N_EXPERTS
    wmap = lambda i, te, nu: (e_base + te[i], 0, 0)
    return pl.pallas_call(
        _ffn_kernel,
        out_shape=jax.ShapeDtypeStruct((npad, d), F32),
        grid_spec=pltpu.PrefetchScalarGridSpec(
            num_scalar_prefetch=2, grid=(npad // tr,),
            in_specs=[pl.BlockSpec((tr, d), lambda i, te, nu: (jnp.minimum(i, nu[0] - 1), 0)),
                      pl.BlockSpec((1, d, D_EXPERT), wmap),
                      pl.BlockSpec((1, d, D_EXPERT), wmap),
                      pl.BlockSpec((1, D_EXPERT, d), wmap)],
            out_specs=pl.BlockSpec((tr, d), lambda i, te, nu: (i, 0)),
            scratch_shapes=[pltpu.VMEM((d, D_EXPERT), BF16), pltpu.VMEM((d, D_EXPERT), BF16),
                            pltpu.VMEM((D_EXPERT, d), BF16)]),
        compiler_params=_cparams(("arbitrary",)),
        name="expert_ffn",
    )(tile_expert, n_used, xg, wg, wu, wd)


def _combine_kernel(d0_ref, d1_ref, x_ref, rf_ref, g2_ref, fg_ref, y_ref, o_ref, buf, sem, *, final):
    tm = x_ref.shape[0]
    base = pl.program_id(0) * tm

    def issue(r, carry):
        _row_copy(y_ref.at[pl.ds(d0_ref[base + r], 1), :], buf.at[0, pl.ds(r, 1), :], sem.at[0]).start(priority=0)
        _row_copy(y_ref.at[pl.ds(d1_ref[base + r], 1), :], buf.at[1, pl.ds(r, 1), :], sem.at[0]).start(priority=1)
        return carry
    lax.fori_loop(0, tm, issue, 0, unroll=DMA_UNROLL)

    def drain(r, carry):
        _row_copy(y_ref.at[pl.ds(d0_ref[base + r], 1), :], buf.at[0, pl.ds(r, 1), :], sem.at[0]).wait()
        _row_copy(y_ref.at[pl.ds(d1_ref[base + r], 1), :], buf.at[1, pl.ds(r, 1), :], sem.at[0]).wait()
        return carry
    lax.fori_loop(0, tm, drain, 0, unroll=DMA_UNROLL)

    w0 = rf_ref[:, 0:1]
    w1 = rf_ref[:, 1:2]
    xn = x_ref[...] + g2_ref[0] * (w0 * buf[0] + w1 * buf[1])
    if final:
        xn = (xn * _rms(xn)) * fg_ref[...]
    o_ref[...] = xn


def _combine(d0, d1, x2, rf, g2, final_g, y, seq, final):
    t, d = x2.shape
    tm = MOVE_ROW_TILE
    per_b = seq // tm
    return pl.pallas_call(
        functools.partial(_combine_kernel, final=final),
        out_shape=jax.ShapeDtypeStruct((t, d), F32),
        grid_spec=pltpu.PrefetchScalarGridSpec(
            num_scalar_prefetch=2, grid=(t // tm,),
            in_specs=[pl.BlockSpec((tm, d), lambda i, a, b: (i, 0)),
                      pl.BlockSpec((tm, LANES), lambda i, a, b: (i, 0)),
                      pl.BlockSpec((1, 1, d), lambda i, a, b: (i // per_b, 0, 0)),
                      pl.BlockSpec((1, d), lambda i, a, b: (0, 0)),
                      pl.BlockSpec(memory_space=pl.ANY)],
            out_specs=pl.BlockSpec((tm, d), lambda i, a, b: (i, 0)),
            scratch_shapes=[pltpu.VMEM((2, tm, d), F32), pltpu.SemaphoreType.DMA((1,))]),
        compiler_params=_cparams(("arbitrary",)),
        name="combine",
    )(d0, d1, x2, rf, g2, final_g, y)


def _routing_plan(ri, cnt, n_tiles):
    tr = EXPERT_ROW_TILE
    counts = cnt[0, :N_EXPERTS].astype(I32)
    tiles_e = (counts + tr - 1) // tr
    tile_end = jnp.cumsum(tiles_e)
    row_off = (tile_end - tiles_e) * tr
    d0 = jnp.take(row_off, ri[:, 0]) + ri[:, 2]
    d1 = jnp.take(row_off, ri[:, 1]) + ri[:, 3]
    n_used = tile_end[-1]
    tid = jnp.minimum(jnp.arange(n_tiles, dtype=I32), n_used - 1)
    tile_expert = jnp.sum((tid[:, None] >= tile_end[None, :]).astype(I32), axis=1)
    return d0, d1, tile_expert.astype(I32), n_used.reshape(1).astype(I32)


def kernel(x, c, rel_bias, w_ada, b_ada, norm1_g, norm2_g, w_qkv, attn_sinks, swa_out_g, moba_out_g,
           w_o, w_group, b_group, w_expert_router, b_expert_router, w_gate, w_up, w_down, final_g):
    b, s, d = x.shape
    depth = w_ada.shape[0]
    t = b * s
    nblk = s // MOBA_BLOCK

    w = SWA_WINDOW
    key = jnp.arange(2 * w, dtype=I32)[:, None]
    swa_dist = (w + jnp.arange(w, dtype=I32))[None, :] - key
    band = (swa_dist >= 0) & (swa_dist < SWA_WINDOW)
    first = jnp.stack([band & (key >= w), band])
    swa_bkt = jnp.where(first, _rel_bucket(swa_dist)[None], REL_BUCKETS)
    n_tab = min(nblk, FAR_TILE + 1)
    pos = jnp.arange(MOBA_BLOCK, dtype=I32)
    moba_dist = (jnp.arange(n_tab, dtype=I32)[:, None, None] * MOBA_BLOCK
                 + pos[None, None, :] - pos[None, :, None])
    moba_bkt = jnp.where(moba_dist >= 0, _rel_bucket(moba_dist), REL_BUCKETS)
    swa_tab = _bias_table(rel_bias, swa_bkt, SWA_Q_HEADS, 0, scale=LOG2E, group=SWA_GROUP)
    moba_tab = _bias_table(rel_bias, moba_bkt, MOBA_HEADS, SWA_Q_HEADS, scale=LOG2E)

    c_pad = jnp.zeros((8, d), F32).at[:b].set(c)
    mod = _ada_mod(c_pad, w_ada, b_ada)[:, :b]

    w_qkv_bf = w_qkv.astype(BF16)
    w_o_bf = w_o.astype(BF16)
    wg_all = w_gate.reshape(depth * N_EXPERTS, d, D_EXPERT)
    wu_all = w_up.reshape(depth * N_EXPERTS, d, D_EXPERT)
    wd_all = w_down.reshape(depth * N_EXPERTS, D_EXPERT, d)

    pad = jnp.zeros((depth, d, LANES - N_GROUPS - N_EXPERTS), F32)
    wr = jnp.concatenate([w_group, w_expert_router, pad], axis=2)
    wr_hi = wr.astype(BF16)
    wr2_all = jnp.concatenate([wr_hi, (wr - wr_hi.astype(F32)).astype(BF16)], axis=2)
    br_all = jnp.concatenate([b_group, b_expert_router, pad[:, 0, :]], axis=1).reshape(depth, 1, LANES)

    n_tiles = (t * 2) // EXPERT_ROW_TILE + N_EXPERTS
    xg = jnp.zeros((n_tiles * EXPERT_ROW_TILE, d), F32)
    x2 = x.reshape(t, d)
    fg = final_g.reshape(1, d)
    for l in range(depth):
        sh1, sc1, g1, sh2, sc2, g2 = [mod[l, :, k * d:(k + 1) * d].reshape(b, 1, d) for k in range(6)]
        qkv = _qkv_proj(x2, sh1, sc1, norm1_g[l].reshape(1, d), w_qkv_bf[l], s)
        qkv3 = qkv.reshape(b, s, D_QKV)
        ya = _swa_attention(qkv3, attn_sinks[l], swa_tab, swa_out_g[l].reshape(1, D_SWA))
        yb = _moba_attention(qkv3, moba_tab)
        x2, h2, logits = _oproj(ya.reshape(t, D_SWA), yb.reshape(t, D_MOBA), x2,
                                moba_out_g[l].reshape(1, D_MOBA), w_o_bf[l], g1,
                                norm2_g[l].reshape(1, d), sc2, sh2, wr2_all[l], br_all[l], s)
        ri, rf, cnt = _route(logits)
        d0, d1, tile_expert, n_used = _routing_plan(ri, cnt, n_tiles)
        xg = _dispatch(d0, d1, h2, xg)
        y = _expert_ffn(tile_expert, n_used, xg, wg_all, wu_all, wd_all, l)
        x2 = _combine(d0, d1, x2, rf, g2, fg, y, s, final=(l == depth - 1))
    return x2.reshape(b, s, d)
```
